```python
import jax, jax.numpy as jnp
from jax import lax
import numpy as np

D_MODEL = 2048
BATCH = 8
SEQ = 2048
DEPTH = 2

CTX_LEN = 256
GRID_W = 64
N_MIXERS = 2
Q_BLOCK = 128
ROPE_THETA = 10000.0
EPS = 1e-6
N_MOD = 6

MLA_HEADS = 16
MLA_Q_RANK = 768
MLA_KV_RANK = 512
MLA_NOPE = 128
MLA_ROPE = 64
MLA_QK = MLA_NOPE + MLA_ROPE
MLA_V = 128

GQA_HEADS = 16
GQA_KV_HEADS = 4
GQA_REP = GQA_HEADS // GQA_KV_HEADS
GQA_HEAD_DIM = 128

N_EXPERTS = 64
N_EXPERT_GROUPS = 8
TOPK_GROUPS = 4
TOP_K = 8
D_EXPERT = 512
D_SHARED = 512
ROUTED_SCALE = 2.5
MOE_BLOCK = 128

N_LAYERS_A = (DEPTH + N_MIXERS - 1) // N_MIXERS
N_LAYERS_B = DEPTH // N_MIXERS

kernel_name = "hybrid_mla_gqa_moe_diffusion_trunk"


def rms_norm(x, gain):
    xf = x.astype(jnp.float32)
    y = xf * lax.rsqrt(jnp.mean(xf * xf, axis=-1, keepdims=True) + EPS)
    return (y * gain.astype(jnp.float32)).astype(x.dtype)


def modulate(h, shift, scale):
    return h * (1.0 + scale) + shift


def axial_rope_tables(n_tokens, rot_dim):
    rows = n_tokens // GRID_W
    row = jnp.repeat(jnp.arange(rows, dtype=jnp.float32), GRID_W)
    col = jnp.tile(jnp.arange(GRID_W, dtype=jnp.float32), rows)
    half = rot_dim // 2
    inv_freq = ROPE_THETA ** (-jnp.arange(0, half, 2, dtype=jnp.float32) / half)
    ang_r = row[:, None] * inv_freq[None, :]
    ang_c = col[:, None] * inv_freq[None, :]
    ang = jnp.concatenate([ang_r, ang_r, ang_c, ang_c], axis=-1)
    return jnp.cos(ang), jnp.sin(ang)


def rotate_axial_halves(x):
    x1, x2, x3, x4 = jnp.split(x, 4, axis=-1)
    return jnp.concatenate([-x2, x1, -x4, x3], axis=-1)


def apply_rope(x, cos, sin):
    shape = (1, cos.shape[0]) + (1,) * (x.ndim - 3) + (cos.shape[-1],)
    c = cos.reshape(shape).astype(x.dtype)
    s = sin.reshape(shape).astype(x.dtype)
    return x * c + rotate_axial_halves(x) * s


def blocked_attention(q, k, v, scale):
    b, t, g, r, dh = q.shape
    nb = t // Q_BLOCK
    qb = jnp.moveaxis(q.reshape(b, nb, Q_BLOCK, g, r, dh), 1, 0)
    kf = k.astype(jnp.float32)
    vf = v.astype(jnp.float32)

    def one_block(q_blk):
        s = jnp.einsum('bqgrd,bkgd->bgrqk', q_blk.astype(jnp.float32), kf) * scale
        p = jax.nn.softmax(s, axis=-1)
        return jnp.einsum('bgrqk,bkgv->bqgrv', p, vf).astype(q.dtype)

    o = lax.map(one_block, qb)
    return jnp.moveaxis(o, 0, 1).reshape(b, t, g, r, v.shape[-1])


def mla_mixer(a_lat, a_ctx, cos, sin, w_down, g_q, g_kv, w_uq, w_ukv, g_qn, g_kn, w_o, with_ctx):
    def project(h, rotate):
        b, t, _ = h.shape
        down = h @ w_down
        cq = rms_norm(down[..., :MLA_Q_RANK], g_q)
        c_kv = rms_norm(down[..., MLA_Q_RANK:MLA_Q_RANK + MLA_KV_RANK], g_kv)
        k_rope = down[..., MLA_Q_RANK + MLA_KV_RANK:]
        q = (cq @ w_uq).reshape(b, t, MLA_HEADS, MLA_QK)
        kv = (c_kv @ w_ukv).reshape(b, t, MLA_HEADS, MLA_NOPE + MLA_V)
        k = jnp.concatenate(
            [kv[..., :MLA_NOPE],
             jnp.broadcast_to(k_rope[:, :, None, :], (b, t, MLA_HEADS, MLA_ROPE))], axis=-1)
        v = kv[..., MLA_NOPE:]
        q, k = rms_norm(q, g_qn), rms_norm(k, g_kn)
        if rotate:
            q = jnp.concatenate([q[..., :MLA_NOPE], apply_rope(q[..., MLA_NOPE:], cos, sin)], axis=-1)
            k = jnp.concatenate([k[..., :MLA_NOPE], apply_rope(k[..., MLA_NOPE:], cos, sin)], axis=-1)
        return q, k, v

    scale = MLA_QK ** -0.5
    b, t = a_lat.shape[:2]
    q_l, k_l, v_l = project(a_lat, True)
    q_c, k_c, v_c = project(a_ctx, False)
    k_all = jnp.concatenate([k_c, k_l], axis=1)
    v_all = jnp.concatenate([v_c, v_l], axis=1)
    o_lat = blocked_attention(q_l[:, :, :, None], k_all, v_all, scale).reshape(b, t, -1) @ w_o
    o_ctx = None
    if with_ctx:
        o_ctx = blocked_attention(q_c[:, :, :, None], k_c, v_c, scale).reshape(b, a_ctx.shape[1], -1) @ w_o
    return o_lat, o_ctx


def gqa_mixer(a_lat, a_ctx, cos, sin, w_qkv, g_qn, g_kn, w_o, with_ctx):
    nq = GQA_HEADS * GQA_HEAD_DIM
    nk = GQA_KV_HEADS * GQA_HEAD_DIM

    def project(h, rotate):
        b, t, _ = h.shape
        qkv = h @ w_qkv
        q = rms_norm(qkv[..., :nq].reshape(b, t, GQA_KV_HEADS, GQA_REP, GQA_HEAD_DIM), g_qn)
        k = rms_norm(qkv[..., nq:nq + nk].reshape(b, t, GQA_KV_HEADS, GQA_HEAD_DIM), g_kn)
        v = qkv[..., nq + nk:].reshape(b, t, GQA_KV_HEADS, GQA_HEAD_DIM)
        if rotate:
            q, k = apply_rope(q, cos, sin), apply_rope(k, cos, sin)
        return q, k, v

    scale = GQA_HEAD_DIM ** -0.5
    b, t = a_lat.shape[:2]
    q_l, k_l, v_l = project(a_lat, True)
    q_c, k_c, v_c = project(a_ctx, False)
    k_all = jnp.concatenate([k_c, k_l], axis=1)
    v_all = jnp.concatenate([v_c, v_l], axis=1)
    o_lat = blocked_attention(q_l, k_all, v_all, scale).reshape(b, t, -1) @ w_o
    o_ctx = None
    if with_ctx:
        o_ctx = blocked_attention(q_c, k_c, v_c, scale).reshape(b, a_ctx.shape[1], -1) @ w_o
    return o_lat, o_ctx


def swiglu(h, w_gate, w_up, w_down):
    return (jax.nn.silu(h @ w_gate) * (h @ w_up)) @ w_down


def route(h, w_router, b_router):
    n = h.shape[0]
    scores = jax.nn.sigmoid(h.astype(jnp.float32) @ w_router.astype(jnp.float32))
    biased = scores + b_router.astype(jnp.float32)
    grouped = biased.reshape(n, N_EXPERT_GROUPS, N_EXPERTS // N_EXPERT_GROUPS)
    group_score = jnp.sum(lax.top_k(grouped, 2)[0], axis=-1)
    _, top_groups = lax.top_k(group_score, TOPK_GROUPS)
    group_mask = jnp.any(top_groups[:, :, None] == jnp.arange(N_EXPERT_GROUPS)[None, None, :], axis=1)
    expert_mask = jnp.repeat(group_mask, N_EXPERTS // N_EXPERT_GROUPS, axis=-1)
    _, idx = lax.top_k(jnp.where(expert_mask, biased, -jnp.inf), TOP_K)
    w = jnp.take_along_axis(scores, idx, axis=-1)
    w = w / jnp.sum(w, axis=-1, keepdims=True) * ROUTED_SCALE
    return idx.astype(jnp.int32), w


def routed_experts(h, idx, wts, w_gate, w_up, w_down):
    n, d = h.shape
    a = n * TOP_K
    e_flat = idx.reshape(-1)
    tok_flat = jnp.repeat(jnp.arange(n, dtype=jnp.int32), TOP_K)
    w_flat = wts.reshape(-1)
    order = jnp.argsort(e_flat)
    e_s, tok_s, w_s = e_flat[order], tok_flat[order], w_flat[order]
    counts = jnp.bincount(e_flat, length=N_EXPERTS).astype(jnp.int32)
    padded = (counts + MOE_BLOCK - 1) // MOE_BLOCK * MOE_BLOCK
    start = jnp.cumsum(counts) - counts
    pad_end = jnp.cumsum(padded)
    pad_start = pad_end - padded
    dest = pad_start[e_s] + jnp.arange(a, dtype=jnp.int32) - start[e_s]
    n_blocks = -(-(a + N_EXPERTS * (MOE_BLOCK - 1)) // MOE_BLOCK)
    cap = n_blocks * MOE_BLOCK
    tok_buf = jnp.full((cap,), n, jnp.int32).at[dest].set(tok_s)
    w_buf = jnp.zeros((cap,), h.dtype).at[dest].set(w_s.astype(h.dtype))
    block_start = jnp.arange(n_blocks, dtype=jnp.int32) * MOE_BLOCK
    block_expert = jnp.minimum(jnp.searchsorted(pad_end, block_start, side='right'), N_EXPERTS - 1)
    h_pad = jnp.concatenate([h, jnp.zeros((1, d), h.dtype)], axis=0)

    def one_block(args):
        toks, wb, e = args
        xb = h_pad[toks]
        return swiglu(xb, w_gate[e], w_up[e], w_down[e]) * wb[:, None]

    y = lax.map(one_block, (tok_buf.reshape(n_blocks, MOE_BLOCK),
                            w_buf.reshape(n_blocks, MOE_BLOCK), block_expert))
    return jnp.zeros((n + 1, d), h.dtype).at[tok_buf].add(y.reshape(cap, d))[:n]


def moe_ffn(h, w_router, b_router, w_gate, w_up, w_down, ws_gate, ws_up, ws_down):
    shp = h.shape
    hf = h.reshape(-1, shp[-1])
    idx, wts = route(hf, w_router, b_router)
    routed = routed_experts(hf, idx, wts, w_gate, w_up, w_down)
    shared = swiglu(hf, ws_gate, ws_up, ws_down)
    return (routed + shared).reshape(shp)


def setup_inputs(seed: int = 0) -> dict:
    key = jax.random.key(seed)
    ks = jax.random.split(key, 32)
    f32 = jnp.float32
    D = D_MODEL

    def nrm(k, shape, fan_in, gain=1.0):
        return jax.random.normal(k, shape, f32) * (gain * fan_in ** -0.5)

    def gain_(k, shape):
        return 1.0 + 0.05 * jax.random.normal(k, shape, f32)

    return {
        "x": jax.random.normal(ks[0], (BATCH, SEQ, D), f32),
        "c": jax.random.normal(ks[1], (BATCH, D), f32),
        "ctx": jax.random.normal(ks[2], (BATCH, CTX_LEN, D), f32),
        "c_ctx": jax.random.normal(ks[3], (D,), f32),
        "ada_w": nrm(ks[4], (DEPTH, D, N_MOD * D), D, 0.5),
        "ada_b": 0.01 * jax.random.normal(ks[5], (DEPTH, N_MOD * D), f32),
        "norm1_g": gain_(ks[6], (DEPTH, D)),
        "norm2_g": gain_(ks[7], (DEPTH, D)),
        "mla_w_down": nrm(ks[8], (N_LAYERS_A, D, MLA_Q_RANK + MLA_KV_RANK + MLA_ROPE), D),
        "mla_g_q": gain_(ks[9], (N_LAYERS_A, MLA_Q_RANK)),
        "mla_g_kv": gain_(ks[10], (N_LAYERS_A, MLA_KV_RANK)),
        "mla_w_uq": nrm(ks[11], (N_LAYERS_A, MLA_Q_RANK, MLA_HEADS * MLA_QK), MLA_Q_RANK),
        "mla_w_ukv": nrm(ks[12], (N_LAYERS_A, MLA_KV_RANK, MLA_HEADS * (MLA_NOPE + MLA_V)), MLA_KV_RANK),
        "mla_g_qn": gain_(ks[13], (N_LAYERS_A, MLA_QK)),
        "mla_g_kn": gain_(ks[14], (N_LAYERS_A, MLA_QK)),
        "mla_w_o": nrm(ks[15], (N_LAYERS_A, MLA_HEADS * MLA_V, D), MLA_HEADS * MLA_V),
        "gqa_w_qkv": nrm(ks[16], (N_LAYERS_B, D, (GQA_HEADS + 2 * GQA_KV_HEADS) * GQA_HEAD_DIM), D),
        "gqa_g_qn": gain_(ks[17], (N_LAYERS_B, GQA_HEAD_DIM)),
        "gqa_g_kn": gain_(ks[18], (N_LAYERS_B, GQA_HEAD_DIM)),
        "gqa_w_o": nrm(ks[19], (N_LAYERS_B, GQA_HEADS * GQA_HEAD_DIM, D), GQA_HEADS * GQA_HEAD_DIM),
        "moe_w_router": nrm(ks[20], (DEPTH, D, N_EXPERTS), D),
        "moe_b_router": 0.01 * jax.random.normal(ks[21], (DEPTH, N_EXPERTS), f32),
        "moe_w_gate": nrm(ks[22], (DEPTH, N_EXPERTS, D, D_EXPERT), D),
        "moe_w_up": nrm(ks[23], (DEPTH, N_EXPERTS, D, D_EXPERT), D),
        "moe_w_down": nrm(ks[24], (DEPTH, N_EXPERTS, D_EXPERT, D), D_EXPERT),
        "moe_ws_gate": nrm(ks[25], (DEPTH, D, D_SHARED), D),
        "moe_ws_up": nrm(ks[26], (DEPTH, D, D_SHARED), D),
        "moe_ws_down": nrm(ks[27], (DEPTH, D_SHARED, D), D_SHARED),
    }


def reference(x, c, ctx, c_ctx, ada_w, ada_b, norm1_g, norm2_g,
              mla_w_down, mla_g_q, mla_g_kv, mla_w_uq, mla_w_ukv, mla_g_qn, mla_g_kn, mla_w_o,
              gqa_w_qkv, gqa_g_qn, gqa_g_kn, gqa_w_o,
              moe_w_router, moe_b_router, moe_w_gate, moe_w_up, moe_w_down,
              moe_ws_gate, moe_ws_up, moe_ws_down):
    n_lat = x.shape[1]
    n_ctx = ctx.shape[1]
    cos_a, sin_a = axial_rope_tables(n_lat, MLA_ROPE)
    cos_b, sin_b = axial_rope_tables(n_lat, GQA_HEAD_DIM)
    h_lat, h_ctx = x, ctx
    for i in range(DEPTH):
        with_ctx = i < DEPTH - 1
        mod_lat = (jax.nn.silu(c) @ ada_w[i] + ada_b[i])[:, None, :]
        mod_ctx = jax.nn.silu(c_ctx) @ ada_w[i] + ada_b[i]
        sh1, sc1, gt1, sh2, sc2, gt2 = jnp.split(mod_lat, N_MOD, axis=-1)
        csh1, csc1, cgt1, csh2, csc2, cgt2 = jnp.split(mod_ctx, N_MOD, axis=-1)

        a_lat = modulate(rms_norm(h_lat, norm1_g[i]), sh1, sc1)
        a_ctx = modulate(rms_norm(h_ctx, norm1_g[i]), csh1, csc1)
        j = i // N_MIXERS
        if i % N_MIXERS == 0:
            o_lat, o_ctx = mla_mixer(a_lat, a_ctx, cos_a, sin_a, mla_w_down[j], mla_g_q[j], mla_g_kv[j],
                                     mla_w_uq[j], mla_w_ukv[j], mla_g_qn[j], mla_g_kn[j], mla_w_o[j],
                                     with_ctx)
        else:
            o_lat, o_ctx = gqa_mixer(a_lat, a_ctx, cos_b, sin_b, gqa_w_qkv[j], gqa_g_qn[j], gqa_g_kn[j],
                                     gqa_w_o[j], with_ctx)
        h_lat = h_lat + gt1 * o_lat

        f_lat = modulate(rms_norm(h_lat, norm2_g[i]), sh2, sc2)
        moe_args = (moe_w_router[i], moe_b_router[i], moe_w_gate[i], moe_w_up[i], moe_w_down[i],
                    moe_ws_gate[i], moe_ws_up[i], moe_ws_down[i])
        if with_ctx:
            h_ctx = h_ctx + cgt1 * o_ctx
            f_ctx = modulate(rms_norm(h_ctx, norm2_g[i]), csh2, csc2)
            y = moe_ffn(jnp.concatenate([f_ctx, f_lat], axis=1), *moe_args)
            h_ctx = h_ctx + cgt2 * y[:, :n_ctx]
            h_lat = h_lat + gt2 * y[:, n_ctx:]
        else:
            h_lat = h_lat + gt2 * moe_ffn(f_lat, *moe_args)
    return h_lat
```

```python
import functools

import jax
import jax.numpy as jnp
import numpy as np
from jax import lax
from jax.experimental import pallas as pl
from jax.experimental.pallas import tpu as pltpu

F32 = jnp.float32
BF16 = jnp.bfloat16
HIGHEST = lax.Precision.HIGHEST

D = 2048
B = 8
T = 2048
CTX = 256
S = CTX + T
N = B * S
TT = 256
TILES_B = S // TT
GRID_W = 64
ROPE_THETA = 10000.0
EPS = 1e-6
N_MOD = 6

MLA_HEADS = 16
MLA_Q_RANK = 768
MLA_KV_RANK = 512
MLA_NOPE = 128
MLA_ROPE = 64
MLA_QK = MLA_NOPE + MLA_ROPE
MLA_V = 128
MLA_HW = 256

GQA_HEADS = 16
GQA_KV_HEADS = 4
GQA_REP = GQA_HEADS // GQA_KV_HEADS
GQA_HD = 128

N_EXPERTS = 64
N_GROUPS = 8
GROUP_SIZE = N_EXPERTS // N_GROUPS
TOPK_GROUPS = 4
TOP_K = 8
D_EXPERT = 512
ROUTED_SCALE = 2.5
MOE_BLOCK = 128

LANES = 128
VMEM_LIMIT = 56 * 1024 * 1024


def _params(sem, vmem=VMEM_LIMIT):
    return pltpu.CompilerParams(dimension_semantics=sem, vmem_limit_bytes=vmem)


def _mod_row(i):
    return jnp.where(i % TILES_B == 0, 0, i // TILES_B + 1)


def _ada_kernel(c_ref, w_ref, b_ref, o_ref):
    a = c_ref[...]
    a = a * jax.nn.sigmoid(a)
    o_ref[...] = jnp.dot(a, w_ref[...], preferred_element_type=F32, precision=HIGHEST) + b_ref[...]


def ada_table(c_rows, ada_w, ada_b):
    depth = ada_w.shape[0]
    tn = 1024
    return pl.pallas_call(
        _ada_kernel,
        grid=(depth, N_MOD * D // tn),
        in_specs=[
            pl.BlockSpec((16, D), lambda l, n: (0, 0)),
            pl.BlockSpec((None, D, tn), lambda l, n: (l, 0, n)),
            pl.BlockSpec((None, 1, tn), lambda l, n: (l, 0, n)),
        ],
        out_specs=pl.BlockSpec((None, 16, tn), lambda l, n: (l, 0, n)),
        out_shape=jax.ShapeDtypeStruct((depth, 16, N_MOD * D), F32),
        compiler_params=_params(("parallel", "parallel")),
        name="ada_table",
    )(c_rows, ada_w, ada_b.reshape(depth, 1, N_MOD * D))


def _norm_mod_kernel(h_ref, g_ref, sh_ref, sc_ref, o_ref):
    x = h_ref[...]
    ms = jnp.mean(x * x, axis=-1, keepdims=True)
    y = x * lax.rsqrt(ms + EPS) * g_ref[...]
    o_ref[...] = (y * (1.0 + sc_ref[...]) + sh_ref[...]).astype(o_ref.dtype)


def norm_mod(h, gain, mod3, shift_idx, scale_idx, out_dtype):
    return pl.pallas_call(
        _norm_mod_kernel,
        grid=(N // TT,),
        in_specs=[
            pl.BlockSpec((TT, D), lambda i: (i, 0)),
            pl.BlockSpec((1, D), lambda i: (0, 0)),
            pl.BlockSpec((None, 1, D), lambda i: (_mod_row(i) * N_MOD + shift_idx, 0, 0)),
            pl.BlockSpec((None, 1, D), lambda i: (_mod_row(i) * N_MOD + scale_idx, 0, 0)),
        ],
        out_specs=pl.BlockSpec((TT, D), lambda i: (i, 0)),
        out_shape=jax.ShapeDtypeStruct((N, D), out_dtype),
        compiler_params=_params(("parallel",)),
        name="norm_mod",
    )(h, gain.reshape(1, D), mod3, mod3)


def _mm_kernel(*refs, n_extra, n_out, epilogue):
    a_ref, w_ref = refs[0], refs[1]
    extra = refs[2:2 + n_extra]
    outs = refs[2 + n_extra:2 + n_extra + n_out]
    wb_ref = refs[2 + n_extra + n_out]

    @pl.when(pl.program_id(1) == 0)
    def _():
        wb_ref[...] = w_ref[...].astype(BF16)

    acc = jnp.dot(a_ref[...].astype(BF16), wb_ref[...], preferred_element_type=F32)
    epilogue(acc, extra, outs)


def matmul(a, w, *, tn, col_off=0, n_cols=None, extra=(), extra_specs=(), out_shapes, out_specs, epilogue,
           name):
    k = a.shape[1]
    n_cols = w.shape[1] if n_cols is None else n_cols
    kern = functools.partial(_mm_kernel, n_extra=len(extra), n_out=len(out_shapes), epilogue=epilogue)
    return pl.pallas_call(
        kern,
        grid=(n_cols // tn, N // TT),
        in_specs=[
            pl.BlockSpec((TT, k), lambda n, i: (i, 0)),
            pl.BlockSpec((k, tn), lambda n, i: (0, n + col_off)),
            *extra_specs,
        ],
        out_specs=out_specs,
        out_shape=out_shapes,
        scratch_shapes=[pltpu.VMEM((k, tn), BF16)],
        compiler_params=_params(("parallel", "arbitrary")),
        name=name,
    )(a, w, *extra)


def _epi_plain(acc, extra, outs):
    outs[0][...] = acc.astype(outs[0].dtype)


def _epi_rmsnorm(acc, extra, outs):
    (g_ref,) = extra
    ms = jnp.mean(acc * acc, axis=-1, keepdims=True)
    outs[0][...] = (acc * lax.rsqrt(ms + EPS) * g_ref[...]).astype(outs[0].dtype)


def _epi_resgate(acc, extra, outs):
    res_ref, gate_ref = extra
    outs[0][...] = res_ref[...] + gate_ref[...] * acc


def _rope(x, c, s1, s2, quarter):
    return x * c + pltpu.roll(x, LANES - quarter, 1) * s1 + pltpu.roll(x, quarter, 1) * s2


def _epi_head_rope(acc, extra, outs, *, n_heads):
    g_ref, c_ref, s1_ref, s2_ref = extra
    c, s1, s2 = c_ref[...], s1_ref[...], s2_ref[...]
    for j in range(n_heads):
        x = acc[:, j * LANES:(j + 1) * LANES]
        ms = jnp.mean(x * x, axis=-1, keepdims=True)
        xn = x * lax.rsqrt(ms + EPS) * g_ref[...]
        outs[0][:, j * LANES:(j + 1) * LANES] = _rope(xn, c, s1, s2, GQA_HD // 4).astype(outs[0].dtype)


def _mla_head(nope, rope, g_ref, c, s1, s2):
    ss = jnp.sum(nope * nope, axis=-1, keepdims=True) + jnp.sum(rope * rope, axis=-1, keepdims=True)
    r = lax.rsqrt(ss * (1.0 / MLA_QK) + EPS)
    nope_n = nope * r * g_ref[:, :LANES]
    rope_n = _rope(rope * r * g_ref[:, LANES:], c, s1, s2, MLA_ROPE // 4)
    return nope_n, rope_n


def _epi_mla_q(acc, extra, outs, *, n_heads):
    g_ref, c_ref, s1_ref, s2_ref = extra
    c, s1, s2 = c_ref[...], s1_ref[...], s2_ref[...]
    for j in range(n_heads):
        nope = acc[:, j * MLA_HW:j * MLA_HW + LANES]
        rope = acc[:, j * MLA_HW + LANES:(j + 1) * MLA_HW]
        nope_n, rope_n = _mla_head(nope, rope, g_ref, c, s1, s2)
        outs[0][:, j * MLA_HW:j * MLA_HW + LANES] = nope_n.astype(outs[0].dtype)
        outs[0][:, j * MLA_HW + LANES:(j + 1) * MLA_HW] = rope_n.astype(outs[0].dtype)


def _epi_mla_kv(acc, extra, outs, *, n_heads):
    kr_ref, g_ref, c_ref, s1_ref, s2_ref = extra
    k_out, v_out = outs
    c, s1, s2 = c_ref[...], s1_ref[...], s2_ref[...]
    rope = kr_ref[...]
    for j in range(n_heads):
        nope = acc[:, j * MLA_HW:j * MLA_HW + LANES]
        v = acc[:, j * MLA_HW + LANES:(j + 1) * MLA_HW]
        nope_n, rope_n = _mla_head(nope, rope, g_ref, c, s1, s2)
        k_out[:, j * MLA_HW:j * MLA_HW + LANES] = nope_n.astype(k_out.dtype)
        k_out[:, j * MLA_HW + LANES:(j + 1) * MLA_HW] = rope_n.astype(k_out.dtype)
        v_out[:, j * LANES:(j + 1) * LANES] = v.astype(v_out.dtype)


def _row_spec(width):
    return pl.BlockSpec((TT, width), lambda n, i: (i, n))


def _const_spec(width):
    return pl.BlockSpec((1, width), lambda n, i: (0, 0))


def _table_spec():
    return pl.BlockSpec((TT, LANES), lambda n, i: (i % TILES_B, 0))


def _attn_kernel(q_ref, k_ref, v_ref, o_ref, *, n_rep, dqk, dv, with_ctx):
    def compute(nk):
        k = k_ref[0:nk, :]
        v = v_ref[0:nk, :]
        for r in range(n_rep):
            q = q_ref[:, r * dqk:(r + 1) * dqk]
            s = lax.dot_general(q, k, (((1,), (1,)), ((), ())), preferred_element_type=F32)
            m = jnp.max(s, axis=-1, keepdims=True)
            p = jnp.exp(s - m)
            l = jnp.sum(p, axis=-1, keepdims=True)
            o = jnp.dot(p.astype(BF16), v, preferred_element_type=F32)
            o_ref[:, r * dv:(r + 1) * dv] = (o / l).astype(o_ref.dtype)

    if with_ctx:
        @pl.when(pl.program_id(2) == 0)
        def _():
            compute(CTX)

        @pl.when(pl.program_id(2) > 0)
        def _():
            compute(S)
    else:
        compute(S)


def attention(q, k, v, *, n_groups, n_rep, dqk, dv, with_ctx):
    first = 0 if with_ctx else 1
    kern = functools.partial(_attn_kernel, n_rep=n_rep, dqk=dqk, dv=dv, with_ctx=with_ctx)
    return pl.pallas_call(
        kern,
        grid=(B, n_groups, TILES_B - first),
        in_specs=[
            pl.BlockSpec((TT, n_rep * dqk), lambda b, g, i: (b * TILES_B + first + i, g)),
            pl.BlockSpec((S, dqk), lambda b, g, i: (b, g)),
            pl.BlockSpec((S, dv), lambda b, g, i: (b, g)),
        ],
        out_specs=pl.BlockSpec((TT, n_rep * dv), lambda b, g, i: (b * TILES_B + first + i, g)),
        out_shape=jax.ShapeDtypeStruct((N, n_groups * n_rep * dv), BF16),
        compiler_params=_params(("parallel", "parallel", "arbitrary")),
        name="attention",
    )(q, k, v)


def _first_index(hit, iota, size):
    return jnp.min(jnp.where(hit, iota, float(size)), axis=0, keepdims=True)


def _router_kernel(f_ref, wr_ref, br_ref, idx_ref, wt_ref, rank_ref, cnt_ref, carry_ref):
    @pl.when(pl.program_id(0) == 0)
    def _():
        carry_ref[...] = jnp.zeros_like(carry_ref)

    neg = -jnp.inf
    logits = lax.dot_general(wr_ref[...], f_ref[...], (((1,), (1,)), ((), ())),
                             preferred_element_type=F32, precision=HIGHEST)
    scores = jax.nn.sigmoid(logits)
    biased = scores + br_ref[...]

    iota_m = lax.broadcasted_iota(jnp.int32, (GROUP_SIZE, TT), 0).astype(F32)
    groups = [biased[g * GROUP_SIZE:(g + 1) * GROUP_SIZE, :] for g in range(N_GROUPS)]
    gs_rows = []
    for blk in groups:
        m1 = jnp.max(blk, axis=0, keepdims=True)
        i1 = _first_index(blk == m1, iota_m, GROUP_SIZE)
        m2 = jnp.max(jnp.where(iota_m == i1, neg, blk), axis=0, keepdims=True)
        gs_rows.append(m1 + m2)
    gs = jnp.concatenate(gs_rows, axis=0)

    iota_g = lax.broadcasted_iota(jnp.int32, gs.shape, 0).astype(F32)
    sel = jnp.zeros(gs.shape, F32)
    cur = gs
    for _ in range(TOPK_GROUPS):
        m = jnp.max(cur, axis=0, keepdims=True)
        hit = iota_g == _first_index(cur == m, iota_g, N_GROUPS)
        sel = jnp.where(hit, 1.0, sel)
        cur = jnp.where(hit, neg, cur)

    cur = jnp.concatenate(
        [jnp.where(sel[g:g + 1, :] > 0.5, groups[g], neg) for g in range(N_GROUPS)], axis=0)
    iota_e = lax.broadcasted_iota(jnp.int32, cur.shape, 0).astype(F32)
    assigned = jnp.zeros(cur.shape, F32)
    w_rows, hits = [], []
    for k in range(TOP_K):
        m = jnp.max(cur, axis=0, keepdims=True)
        first = _first_index(cur == m, iota_e, N_EXPERTS)
        hit = iota_e == first
        idx_ref[k:k + 1, :] = first.astype(jnp.int32)
        w_rows.append(jnp.sum(jnp.where(hit, scores, 0.0), axis=0, keepdims=True))
        hits.append(hit)
        assigned = jnp.where(hit, 1.0, assigned)
        cur = jnp.where(hit, neg, cur)

    w_sum = w_rows[0]
    for k in range(1, TOP_K):
        w_sum = w_sum + w_rows[k]
    for k in range(TOP_K):
        wt_ref[k:k + 1, :] = w_rows[k] / w_sum * ROUTED_SCALE

    r_i = lax.broadcasted_iota(jnp.int32, (TT, TT), 0)
    c_i = lax.broadcasted_iota(jnp.int32, (TT, TT), 1)
    upper = jnp.where(r_i <= c_i, 1.0, 0.0).astype(BF16)
    incl = jnp.dot(assigned.astype(BF16), upper, preferred_element_type=F32)
    rank_e = carry_ref[...] + incl - assigned
    for k in range(TOP_K):
        rank_k = jnp.sum(jnp.where(hits[k], rank_e, 0.0), axis=0, keepdims=True)
        rank_ref[k:k + 1, :] = rank_k.astype(jnp.int32)
    carry = carry_ref[...] + jnp.sum(assigned, axis=1, keepdims=True)
    carry_ref[...] = carry
    cnt_ref[...] = carry.astype(jnp.int32)


def router(f, w_router, b_router):
    tok = pl.BlockSpec((TOP_K, TT), lambda i: (0, i))
    return pl.pallas_call(
        _router_kernel,
        grid=(N // TT,),
        in_specs=[
            pl.BlockSpec((TT, D), lambda i: (i, 0)),
            pl.BlockSpec((N_EXPERTS, D), lambda i: (0, 0)),
            pl.BlockSpec((N_EXPERTS, 1), lambda i: (0, 0)),
        ],
        out_specs=[tok, tok, tok, pl.BlockSpec((N_EXPERTS, 1), lambda i: (0, 0))],
        out_shape=[
            jax.ShapeDtypeStruct((TOP_K, N), jnp.int32),
            jax.ShapeDtypeStruct((TOP_K, N), F32),
            jax.ShapeDtypeStruct((TOP_K, N), jnp.int32),
            jax.ShapeDtypeStruct((N_EXPERTS, 1), jnp.int32),
        ],
        scratch_shapes=[pltpu.VMEM((N_EXPERTS, 1), F32)],
        compiler_params=_params(("arbitrary",)),
        name="router",
    )(f, w_router.T, b_router.reshape(N_EXPERTS, 1))


def _swiglu(x, wg, wu, wd):
    g = jnp.dot(x, wg, preferred_element_type=F32)
    u = jnp.dot(x, wu, preferred_element_type=F32)
    mid = (g * jax.nn.sigmoid(g) * u).astype(BF16)
    return jnp.dot(mid, wd, preferred_element_type=F32)


def _shared_kernel(x_ref, wg_ref, wu_ref, wd_ref, y_ref, wgb, wub, wdb):
    @pl.when(pl.program_id(0) == 0)
    def _():
        wgb[...] = wg_ref[...].astype(BF16)
        wub[...] = wu_ref[...].astype(BF16)
        wdb[...] = wd_ref[...].astype(BF16)

    y_ref[...] = _swiglu(x_ref[...].astype(BF16), wgb[...], wub[...], wdb[...])


def shared_expert(f, wg, wu, wd):
    dh = wg.shape[1]
    return pl.pallas_call(
        _shared_kernel,
        grid=(N // TT,),
        in_specs=[
            pl.BlockSpec((TT, D), lambda i: (i, 0)),
            pl.BlockSpec((D, dh), lambda i: (0, 0)),
            pl.BlockSpec((D, dh), lambda i: (0, 0)),
            pl.BlockSpec((dh, D), lambda i: (0, 0)),
        ],
        out_specs=pl.BlockSpec((TT, D), lambda i: (i, 0)),
        out_shape=jax.ShapeDtypeStruct((N, D), F32),
        scratch_shapes=[pltpu.VMEM((D, dh), BF16), pltpu.VMEM((D, dh), BF16), pltpu.VMEM((dh, D), BF16)],
        compiler_params=_params(("arbitrary",)),
        name="shared_expert",
    )(f, wg, wu, wd)


def _gather_rows(src_hbm, idx_smem, dst, sem, n_rows, idx_off=0):
    def issue(r, carry):
        t = idx_smem[0, idx_off + r]
        pltpu.make_async_copy(src_hbm.at[pl.ds(t, 1)], dst.at[pl.ds(r, 1)], sem).start()
        return carry

    lax.fori_loop(0, n_rows, issue, 0)


def _routed_kernel(be_ref, tok_hbm, f_hbm, wg_ref, wu_ref, wd_ref, y_ref,
                   idx_smem, xbuf, wgb, wub, wdb, sem_i, sem_g):
    j = pl.program_id(0)
    cp = pltpu.make_async_copy(tok_hbm.at[j], idx_smem, sem_i)
    cp.start()
    cp.wait()
    _gather_rows(f_hbm, idx_smem, xbuf, sem_g, MOE_BLOCK)

    changed = jnp.logical_or(j == 0, be_ref[j] != be_ref[jnp.maximum(j - 1, 0)])

    @pl.when(changed)
    def _():
        wgb[...] = wg_ref[...].astype(BF16)
        wub[...] = wu_ref[...].astype(BF16)
        wdb[...] = wd_ref[...].astype(BF16)

    pltpu.make_async_copy(f_hbm.at[pl.ds(0, MOE_BLOCK)], xbuf, sem_g).wait()
    y_ref[...] = _swiglu(xbuf[...].astype(BF16), wgb[...], wub[...], wdb[...])


def routed_experts(f, tok_buf, block_expert, wg, wu, wd):
    n_blocks = block_expert.shape[0]
    dh = wg.shape[2]
    grid_spec = pltpu.PrefetchScalarGridSpec(
        num_scalar_prefetch=1,
        grid=(n_blocks,),
        in_specs=[
            pl.BlockSpec(memory_space=pl.ANY),
            pl.BlockSpec(memory_space=pl.ANY),
            pl.BlockSpec((None, D, dh), lambda j, be: (be[j], 0, 0)),
            pl.BlockSpec((None, D, dh), lambda j, be: (be[j], 0, 0)),
            pl.BlockSpec((None, dh, D), lambda j, be: (be[j], 0, 0)),
        ],
        out_specs=pl.BlockSpec((MOE_BLOCK, D), lambda j, be: (j, 0)),
        scratch_shapes=[
            pltpu.SMEM((1, MOE_BLOCK), jnp.int32),
            pltpu.VMEM((MOE_BLOCK, D), F32),
            pltpu.VMEM((D, dh), BF16), pltpu.VMEM((D, dh), BF16), pltpu.VMEM((dh, D), BF16),
            pltpu.SemaphoreType.DMA, pltpu.SemaphoreType.DMA,
        ],
    )
    return pl.pallas_call(
        _routed_kernel,
        grid_spec=grid_spec,
        out_shape=jax.ShapeDtypeStruct((n_blocks * MOE_BLOCK, D), F32),
        compiler_params=_params(("arbitrary",)),
        name="routed_experts",
    )(block_expert, tok_buf, f, wg, wu, wd)


COMBINE_T = 128


def _combine_kernel(dest_hbm, ys_hbm, wt_ref, sh_ref, h_ref, gate_ref, o_ref, idx_smem, gbuf, sem_i, sem_g):
    i = pl.program_id(0)
    cp = pltpu.make_async_copy(dest_hbm.at[i], idx_smem, sem_i)
    cp.start()
    cp.wait()
    for k in range(TOP_K):
        _gather_rows(ys_hbm, idx_smem, gbuf.at[k], sem_g, COMBINE_T, idx_off=k * COMBINE_T)
    acc = sh_ref[...]
    for k in range(TOP_K):
        pltpu.make_async_copy(ys_hbm.at[pl.ds(0, COMBINE_T)], gbuf.at[k], sem_g).wait()
    for k in range(TOP_K):
        acc = acc + wt_ref[:, k:k + 1] * gbuf[k]
    o_ref[...] = h_ref[...] + gate_ref[...] * acc


def combine(dest, ys, wts, shared, h, mod3, gate_idx):
    per_tt = TT // COMBINE_T
    return pl.pallas_call(
        _combine_kernel,
        grid=(N // COMBINE_T,),
        in_specs=[
            pl.BlockSpec(memory_space=pl.ANY),
            pl.BlockSpec(memory_space=pl.ANY),
            pl.BlockSpec((COMBINE_T, TOP_K), lambda i: (i, 0)),
            pl.BlockSpec((COMBINE_T, D), lambda i: (i, 0)),
            pl.BlockSpec((COMBINE_T, D), lambda i: (i, 0)),
            pl.BlockSpec((None, 1, D), lambda i: (_mod_row(i // per_tt) * N_MOD + gate_idx, 0, 0)),
        ],
        out_specs=pl.BlockSpec((COMBINE_T, D), lambda i: (i, 0)),
        out_shape=jax.ShapeDtypeStruct((N, D), F32),
        scratch_shapes=[
            pltpu.SMEM((1, TOP_K * COMBINE_T), jnp.int32),
            pltpu.VMEM((TOP_K, COMBINE_T, D), F32),
            pltpu.SemaphoreType.DMA, pltpu.SemaphoreType.DMA,
        ],
        compiler_params=_params(("arbitrary",)),
        name="combine",
    )(dest, ys, wts, shared, h, mod3)


def moe_layer(h, f, mod3, gate_idx, w_router, b_router, wg, wu, wd, wsg, wsu, wsd):
    idx, wts, rank, counts = router(f, w_router, b_router)
    counts = counts[:, 0]
    padded = (counts + MOE_BLOCK - 1) // MOE_BLOCK * MOE_BLOCK
    pad_end = jnp.cumsum(padded)
    pad_start = pad_end - padded
    dest = pad_start[idx] + rank
    n_blocks = -(-(N * TOP_K + N_EXPERTS * (MOE_BLOCK - 1)) // MOE_BLOCK)
    cap = n_blocks * MOE_BLOCK
    tok = jnp.broadcast_to(jnp.arange(N, dtype=jnp.int32)[None, :], dest.shape)
    tok_buf = jnp.zeros((cap,), jnp.int32).at[dest.reshape(-1)].set(tok.reshape(-1))
    block_start = jnp.arange(n_blocks, dtype=jnp.int32) * MOE_BLOCK
    block_expert = jnp.minimum(jnp.searchsorted(pad_end, block_start, side='right'),
                               N_EXPERTS - 1).astype(jnp.int32)
    ys = routed_experts(f, tok_buf.reshape(n_blocks, 1, MOE_BLOCK), block_expert, wg, wu, wd)
    shared = shared_expert(f, wsg, wsu, wsd)
    dest_tiles = dest.reshape(TOP_K, N // COMBINE_T, COMBINE_T).transpose(1, 0, 2)
    dest_tiles = dest_tiles.reshape(N // COMBINE_T, 1, TOP_K * COMBINE_T)
    return combine(dest_tiles, ys, wts.T, shared, h, mod3, gate_idx)


def _rope_tables(rot_dim):
    rows = T // GRID_W
    row = np.repeat(np.arange(rows, dtype=np.float32), GRID_W)
    col = np.tile(np.arange(GRID_W, dtype=np.float32), rows)
    half = rot_dim // 2
    inv_freq = jnp.asarray(ROPE_THETA, F32) ** (-jnp.arange(0, half, 2, dtype=F32) / half)
    ang_r = jnp.asarray(row)[:, None] * inv_freq[None, :]
    ang_c = jnp.asarray(col)[:, None] * inv_freq[None, :]
    ang = jnp.concatenate([ang_r, ang_r, ang_c, ang_c], axis=-1)
    cos, sin = jnp.cos(ang), jnp.sin(ang)
    quarter = (np.arange(rot_dim) // (rot_dim // 4)) % 2
    s1 = jnp.where(quarter[None, :] == 0, -sin, 0.0)
    s2 = jnp.where(quarter[None, :] == 1, sin, 0.0)

    def full(tbl, fill):
        tbl = jnp.pad(tbl, ((0, 0), (0, LANES - rot_dim)), constant_values=fill)
        return jnp.concatenate([jnp.full((CTX, LANES), fill, F32), tbl], axis=0)

    return full(cos, 1.0), full(s1, 0.0), full(s2, 0.0)


def _pad_heads(w, n_heads, width, new_width):
    k = w.shape[0]
    w = w.reshape(k, n_heads, width)
    return jnp.pad(w, ((0, 0), (0, 0), (0, new_width - width))).reshape(k, n_heads * new_width)


def _pad_gain(g, new_width):
    return jnp.pad(g, (0, new_width - g.shape[0])).reshape(1, new_width)


def mla_mixer(a, tables, w_down, g_q, g_kv, w_uq, w_ukv, g_qn, g_kn, w_o_args):
    c, s1, s2 = tables
    w_dq = w_down[:, :MLA_Q_RANK]
    w_dkv = w_down[:, MLA_Q_RANK:MLA_Q_RANK + MLA_KV_RANK]
    w_dr = jnp.pad(w_down[:, MLA_Q_RANK + MLA_KV_RANK:], ((0, 0), (0, LANES - MLA_ROPE)))
    cq = matmul(a, w_dq, tn=MLA_Q_RANK, extra=(g_q.reshape(1, -1),), extra_specs=(_const_spec(MLA_Q_RANK),),
                out_shapes=[jax.ShapeDtypeStruct((N, MLA_Q_RANK), BF16)], out_specs=[_row_spec(MLA_Q_RANK)],
                epilogue=_epi_rmsnorm, name="mla_down_q")[0]
    ckv = matmul(a, w_dkv, tn=MLA_KV_RANK, extra=(g_kv.reshape(1, -1),),
                 extra_specs=(_const_spec(MLA_KV_RANK),),
                 out_shapes=[jax.ShapeDtypeStruct((N, MLA_KV_RANK), BF16)], out_specs=[_row_spec(MLA_KV_RANK)],
                 epilogue=_epi_rmsnorm, name="mla_down_kv")[0]
    k_rope = matmul(a, w_dr, tn=LANES, out_shapes=[jax.ShapeDtypeStruct((N, LANES), F32)],
                    out_specs=[_row_spec(LANES)], epilogue=_epi_plain, name="mla_down_rope")[0]

    tn = 1024
    heads_t = tn // MLA_HW
    scale = MLA_QK ** -0.5
    w_uq_p = _pad_heads(w_uq, MLA_HEADS, MLA_QK, MLA_HW)
    q = matmul(cq, w_uq_p, tn=tn, extra=(_pad_gain(g_qn * scale, MLA_HW), c, s1, s2),
               extra_specs=(_const_spec(MLA_HW), _table_spec(), _table_spec(), _table_spec()),
               out_shapes=[jax.ShapeDtypeStruct((N, MLA_HEADS * MLA_HW), BF16)], out_specs=[_row_spec(tn)],
               epilogue=functools.partial(_epi_mla_q, n_heads=heads_t), name="mla_up_q")[0]
    k, v = matmul(ckv, w_ukv, tn=tn, extra=(k_rope, _pad_gain(g_kn, MLA_HW), c, s1, s2),
                  extra_specs=(pl.BlockSpec((TT, LANES), lambda n, i: (i, 0)), _const_spec(MLA_HW),
                               _table_spec(), _table_spec(), _table_spec()),
                  out_shapes=[jax.ShapeDtypeStruct((N, MLA_HEADS * MLA_HW), BF16),
                              jax.ShapeDtypeStruct((N, MLA_HEADS * MLA_V), BF16)],
                  out_specs=[_row_spec(tn), _row_spec(tn // 2)],
                  epilogue=functools.partial(_epi_mla_kv, n_heads=heads_t), name="mla_up_kv")
    o = attention(q, k, v, n_groups=MLA_HEADS, n_rep=1, dqk=MLA_HW, dv=MLA_V, with_ctx=True)
    return _out_proj(o, *w_o_args)


def gqa_mixer(a, tables, w_qkv, g_qn, g_kn, w_o_args):
    c, s1, s2 = tables
    tn = 512
    heads_t = tn // GQA_HD
    scale = GQA_HD ** -0.5
    rope_specs = (_const_spec(GQA_HD), _table_spec(), _table_spec(), _table_spec())
    nq = GQA_HEADS * GQA_HD
    nk = GQA_KV_HEADS * GQA_HD
    q = matmul(a, w_qkv, tn=tn, n_cols=nq, extra=((g_qn * scale).reshape(1, -1), c, s1, s2),
               extra_specs=rope_specs, out_shapes=[jax.ShapeDtypeStruct((N, nq), BF16)],
               out_specs=[_row_spec(tn)], epilogue=functools.partial(_epi_head_rope, n_heads=heads_t),
               name="gqa_q")[0]
    k = matmul(a, w_qkv, tn=tn, col_off=nq // tn, n_cols=nk, extra=(g_kn.reshape(1, -1), c, s1, s2),
               extra_specs=rope_specs, out_shapes=[jax.ShapeDtypeStruct((N, nk), BF16)],
               out_specs=[_row_spec(tn)], epilogue=functools.partial(_epi_head_rope, n_heads=heads_t),
               name="gqa_k")[0]
    v = matmul(a, w_qkv, tn=tn, col_off=(nq + nk) // tn, n_cols=nk,
               out_shapes=[jax.ShapeDtypeStruct((N, nk), BF16)], out_specs=[_row_spec(tn)],
               epilogue=_epi_plain, name="gqa_v")[0]
    o = attention(q, k, v, n_groups=GQA_KV_HEADS, n_rep=GQA_REP, dqk=GQA_HD, dv=GQA_HD, with_ctx=True)
    return _out_proj(o, *w_o_args)


def _out_proj(o, w_o, h, mod3, gate_idx):
    tn = 1024
    return matmul(o, w_o, tn=tn, extra=(h, mod3),
                  extra_specs=(_row_spec(tn),
                               pl.BlockSpec((None, 1, tn), lambda n, i: (_mod_row(i) * N_MOD + gate_idx, 0, n))),
                  out_shapes=[jax.ShapeDtypeStruct((N, D), F32)], out_specs=[_row_spec(tn)],
                  epilogue=_epi_resgate, name="out_proj")[0]


def kernel(x, c, ctx, c_ctx, ada_w, ada_b, norm1_g, norm2_g, mla_w_down, mla_g_q, mla_g_kv, mla_w_uq,
           mla_w_ukv, mla_g_qn, mla_g_kn, mla_w_o, gqa_w_qkv, gqa_g_qn, gqa_g_kn, gqa_w_o, moe_w_router,
           moe_b_router, moe_w_gate, moe_w_up, moe_w_down, moe_ws_gate, moe_ws_up, moe_ws_down):
    depth = ada_w.shape[0]
    h = jnp.concatenate([ctx, x], axis=1).reshape(N, D)
    c_rows = jnp.concatenate([c_ctx[None, :], c, jnp.zeros((16 - 1 - B, D), F32)], axis=0)
    mod = ada_table(c_rows, ada_w, ada_b)
    tables_a = _rope_tables(MLA_ROPE)
    tables_b = _rope_tables(GQA_HD)
    for i in range(depth):
        mod3 = mod[i].reshape(16 * N_MOD, 1, D)
        a = norm_mod(h, norm1_g[i], mod3, 0, 1, BF16)
        j = i // 2
        if i % 2 == 0:
            h = mla_mixer(a, tables_a, mla_w_down[j], mla_g_q[j], mla_g_kv[j], mla_w_uq[j], mla_w_ukv[j],
                          mla_g_qn[j], mla_g_kn[j], (mla_w_o[j], h, mod3, 2))
        else:
            h = gqa_mixer(a, tables_b, gqa_w_qkv[j], gqa_g_qn[j], gqa_g_kn[j], (gqa_w_o[j], h, mod3, 2))
        f = norm_mod(h, norm2_g[i], mod3, 3, 4, F32)
        h = moe_layer(h, f, mod3, 5, moe_w_router[i], moe_b_router[i], moe_w_gate[i], moe_w_up[i],
                      moe_w_down[i], moe_ws_gate[i], moe_ws_up[i], moe_ws_down[i])
    return h.reshape(B, S, D)[:, CTX:, :]
```

```python
import functools

import jax
import jax.numpy as jnp
import numpy as np
from jax import lax
from jax.experimental import pallas as pl
from jax.experimental.pallas import tpu as pltpu

F32 = jnp.float32
BF16 = jnp.bfloat16
HIGHEST = lax.Precision.HIGHEST

D = 2048
B = 8
T = 2048
CTX = 256
S = CTX + T
N = B * S
TT = 256
TILES_B = S // TT
GRID_W = 64
ROPE_THETA = 10000.0
EPS = 1e-6
N_MOD = 6

MLA_HEADS = 16
MLA_Q_RANK = 768
MLA_KV_RANK = 512
MLA_NOPE = 128
MLA_ROPE = 64
MLA_QK = MLA_NOPE + MLA_ROPE
MLA_V = 128
MLA_HW = 256

GQA_HEADS = 16
GQA_KV_HEADS = 4
GQA_REP = GQA_HEADS // GQA_KV_HEADS
GQA_HD = 128

N_EXPERTS = 64
N_GROUPS = 8
GROUP_SIZE = N_EXPERTS // N_GROUPS
TOPK_GROUPS = 4
TOP_K = 8
D_EXPERT = 512
ROUTED_SCALE = 2.5
MOE_BLOCK = 128

LANES = 128
LOG2E = 1.4426950408889634
VMEM_LIMIT = 56 * 1024 * 1024


def _params(sem, vmem=VMEM_LIMIT):
    return pltpu.CompilerParams(dimension_semantics=sem, vmem_limit_bytes=vmem)


def _mod_row(i):
    return jnp.where(i % TILES_B == 0, 0, i // TILES_B + 1)


def _ada_kernel(c_ref, w_ref, b_ref, o_ref):
    a = c_ref[...]
    a = a * jax.nn.sigmoid(a)
    o_ref[...] = jnp.dot(a, w_ref[...], preferred_element_type=F32, precision=HIGHEST) + b_ref[...]


def ada_table(c_rows, ada_w, ada_b):
    depth = ada_w.shape[0]
    tn = 1024
    return pl.pallas_call(
        _ada_kernel,
        grid=(depth, N_MOD * D // tn),
        in_specs=[
            pl.BlockSpec((16, D), lambda l, n: (0, 0)),
            pl.BlockSpec((None, D, tn), lambda l, n: (l, 0, n)),
            pl.BlockSpec((None, 1, tn), lambda l, n: (l, 0, n)),
        ],
        out_specs=pl.BlockSpec((None, 16, tn), lambda l, n: (l, 0, n)),
        out_shape=jax.ShapeDtypeStruct((depth, 16, N_MOD * D), F32),
        compiler_params=_params(("parallel", "parallel")),
        name="ada_table",
    )(c_rows, ada_w, ada_b.reshape(depth, 1, N_MOD * D))


HALF = D // 2
ROW_TILES = HALF // LANES


def _store_packed(ref, y):
    rows = y.shape[0]
    for s in range(ROW_TILES):
        lo = y[:, s * LANES:(s + 1) * LANES]
        hi = y[:, HALF + s * LANES:HALF + (s + 1) * LANES]
        ref[pl.ds(s, rows, stride=ROW_TILES), :] = pltpu.pack_elementwise([lo, hi], packed_dtype=BF16)


def _load_packed(ref, base, rows, s):
    u = ref[pl.ds(base * ROW_TILES + s, rows, stride=ROW_TILES), :]
    lo = pltpu.unpack_elementwise(u, index=0, packed_dtype=BF16, unpacked_dtype=F32)
    hi = pltpu.unpack_elementwise(u, index=1, packed_dtype=BF16, unpacked_dtype=F32)
    return lo, hi


def _norm_mod_kernel(h_ref, g_ref, sh_ref, sc_ref, o_ref, *packed_ref):
    x = h_ref[...]
    ms = jnp.mean(x * x, axis=-1, keepdims=True)
    y = x * lax.rsqrt(ms + EPS) * g_ref[...]
    y = y * (1.0 + sc_ref[...]) + sh_ref[...]
    o_ref[...] = y.astype(o_ref.dtype)
    if packed_ref:
        _store_packed(packed_ref[0], y)


def norm_mod(h, gain, mod3, shift_idx, scale_idx, out_dtype, with_packed=False):
    row = pl.BlockSpec((TT, D), lambda i: (i, 0))
    out_specs, out_shape = [row], [jax.ShapeDtypeStruct((N, D), out_dtype)]
    if with_packed:
        out_specs.append(pl.BlockSpec((TT * ROW_TILES, LANES), lambda i: (i, 0)))
        out_shape.append(jax.ShapeDtypeStruct((N * ROW_TILES, LANES), jnp.uint32))
    outs = pl.pallas_call(
        _norm_mod_kernel,
        grid=(N // TT,),
        in_specs=[
            row,
            pl.BlockSpec((1, D), lambda i: (0, 0)),
            pl.BlockSpec((None, 1, D), lambda i: (_mod_row(i) * N_MOD + shift_idx, 0, 0)),
            pl.BlockSpec((None, 1, D), lambda i: (_mod_row(i) * N_MOD + scale_idx, 0, 0)),
        ],
        out_specs=out_specs,
        out_shape=out_shape,
        compiler_params=_params(("parallel",)),
        name="norm_mod",
    )(h, gain.reshape(1, D), mod3, mod3)
    return outs if with_packed else outs[0]


def _mm_kernel(*refs, n_extra, n_out, epilogue):
    a_ref, w_ref = refs[0], refs[1]
    extra = refs[2:2 + n_extra]
    outs = refs[2 + n_extra:2 + n_extra + n_out]
    wb_ref = refs[2 + n_extra + n_out]

    @pl.when(pl.program_id(1) == 0)
    def _():
        wb_ref[...] = w_ref[...].astype(BF16)

    acc = jnp.dot(a_ref[...].astype(BF16), wb_ref[...], preferred_element_type=F32)
    epilogue(acc, extra, outs)


def matmul(a, w, *, tn, col_off=0, n_cols=None, extra=(), extra_specs=(), out_shapes, out_specs, epilogue,
           name):
    k = a.shape[1]
    n_cols = w.shape[1] if n_cols is None else n_cols
    kern = functools.partial(_mm_kernel, n_extra=len(extra), n_out=len(out_shapes), epilogue=epilogue)
    return pl.pallas_call(
        kern,
        grid=(n_cols // tn, N // TT),
        in_specs=[
            pl.BlockSpec((TT, k), lambda n, i: (i, 0)),
            pl.BlockSpec((k, tn), lambda n, i: (0, n + col_off)),
            *extra_specs,
        ],
        out_specs=out_specs,
        out_shape=out_shapes,
        scratch_shapes=[pltpu.VMEM((k, tn), BF16)],
        compiler_params=_params(("parallel", "arbitrary")),
        name=name,
    )(a, w, *extra)


def _epi_plain(acc, extra, outs):
    outs[0][...] = acc.astype(outs[0].dtype)


def _epi_rmsnorm(acc, extra, outs):
    (g_ref,) = extra
    ms = jnp.mean(acc * acc, axis=-1, keepdims=True)
    outs[0][...] = (acc * lax.rsqrt(ms + EPS) * g_ref[...]).astype(outs[0].dtype)


def _epi_resgate(acc, extra, outs):
    res_ref, gate_ref = extra
    outs[0][...] = res_ref[...] + gate_ref[...] * acc


def _rope(x, c, s1, s2, quarter):
    return x * c + pltpu.roll(x, LANES - quarter, 1) * s1 + pltpu.roll(x, quarter, 1) * s2


def _epi_head_rope(acc, extra, outs, *, n_heads):
    g_ref, c_ref, s1_ref, s2_ref = extra
    c, s1, s2 = c_ref[...], s1_ref[...], s2_ref[...]
    for j in range(n_heads):
        x = acc[:, j * LANES:(j + 1) * LANES]
        ms = jnp.mean(x * x, axis=-1, keepdims=True)
        xn = x * lax.rsqrt(ms + EPS) * g_ref[...]
        outs[0][:, j * LANES:(j + 1) * LANES] = _rope(xn, c, s1, s2, GQA_HD // 4).astype(outs[0].dtype)


def _mla_head(nope, rope, g_ref, c, s1, s2):
    ss = jnp.sum(nope * nope, axis=-1, keepdims=True) + jnp.sum(rope * rope, axis=-1, keepdims=True)
    r = lax.rsqrt(ss * (1.0 / MLA_QK) + EPS)
    nope_n = nope * r * g_ref[:, :LANES]
    rope_n = _rope(rope * r * g_ref[:, LANES:], c, s1, s2, MLA_ROPE // 4)
    return nope_n, rope_n


def _epi_mla_q(acc, extra, outs, *, n_heads):
    g_ref, c_ref, s1_ref, s2_ref = extra
    c, s1, s2 = c_ref[...], s1_ref[...], s2_ref[...]
    for j in range(n_heads):
        nope = acc[:, j * MLA_HW:j * MLA_HW + LANES]
        rope = acc[:, j * MLA_HW + LANES:(j + 1) * MLA_HW]
        nope_n, rope_n = _mla_head(nope, rope, g_ref, c, s1, s2)
        outs[0][:, j * MLA_HW:j * MLA_HW + LANES] = nope_n.astype(outs[0].dtype)
        outs[0][:, j * MLA_HW + LANES:(j + 1) * MLA_HW] = rope_n.astype(outs[0].dtype)


def _epi_mla_kv(acc, extra, outs, *, n_heads):
    kr_ref, g_ref, c_ref, s1_ref, s2_ref = extra
    k_out, v_out = outs
    c, s1, s2 = c_ref[...], s1_ref[...], s2_ref[...]
    rope = kr_ref[...]
    for j in range(n_heads):
        nope = acc[:, j * MLA_HW:j * MLA_HW + LANES]
        v = acc[:, j * MLA_HW + LANES:(j + 1) * MLA_HW]
        nope_n, rope_n = _mla_head(nope, rope, g_ref, c, s1, s2)
        k_out[:, j * MLA_HW:j * MLA_HW + LANES] = nope_n.astype(k_out.dtype)
        k_out[:, j * MLA_HW + LANES:(j + 1) * MLA_HW] = rope_n.astype(k_out.dtype)
        v_out[:, j * LANES:(j + 1) * LANES] = v.astype(v_out.dtype)


def _row_spec(width):
    return pl.BlockSpec((TT, width), lambda n, i: (i, n))


def _const_spec(width):
    return pl.BlockSpec((1, width), lambda n, i: (0, 0))


def _table_spec():
    return pl.BlockSpec((TT, LANES), lambda n, i: (i % TILES_B, 0))


def _attn_kernel(q_ref, k_ref, v_ref, o_ref, *, n_rep, dqk, dv, with_ctx):
    def compute(nk):
        k = k_ref[0:nk, :]
        v = v_ref[0:nk, :]
        for r in range(n_rep):
            q = q_ref[:, r * dqk:(r + 1) * dqk]
            s = lax.dot_general(q, k, (((1,), (1,)), ((), ())), preferred_element_type=F32)
            m = jnp.max(s, axis=-1, keepdims=True)
            p = jnp.exp2(s - m)
            l = jnp.sum(p, axis=-1, keepdims=True)
            o = jnp.dot(p.astype(BF16), v, preferred_element_type=F32)
            o_ref[:, r * dv:(r + 1) * dv] = (o / l).astype(o_ref.dtype)

    if with_ctx:
        @pl.when(pl.program_id(2) == 0)
        def _():
            compute(CTX)

        @pl.when(pl.program_id(2) > 0)
        def _():
            compute(S)
    else:
        compute(S)


def attention(q, k, v, *, n_groups, n_rep, dqk, dv, with_ctx):
    first = 0 if with_ctx else 1
    kern = functools.partial(_attn_kernel, n_rep=n_rep, dqk=dqk, dv=dv, with_ctx=with_ctx)
    return pl.pallas_call(
        kern,
        grid=(B, n_groups, TILES_B - first),
        in_specs=[
            pl.BlockSpec((TT, n_rep * dqk), lambda b, g, i: (b * TILES_B + first + i, g)),
            pl.BlockSpec((S, dqk), lambda b, g, i: (b, g)),
            pl.BlockSpec((S, dv), lambda b, g, i: (b, g)),
        ],
        out_specs=pl.BlockSpec((TT, n_rep * dv), lambda b, g, i: (b * TILES_B + first + i, g)),
        out_shape=jax.ShapeDtypeStruct((N, n_groups * n_rep * dv), BF16),
        compiler_params=_params(("parallel", "parallel", "arbitrary")),
        name="attention",
    )(q, k, v)


def _first_index(hit, iota, size):
    return jnp.min(jnp.where(hit, iota, float(size)), axis=0, keepdims=True)


def _router_kernel(f_ref, wr_ref, br_ref, idx_ref, wt_ref, rank_ref, cnt_ref, carry_ref):
    @pl.when(pl.program_id(0) == 0)
    def _():
        carry_ref[...] = jnp.zeros_like(carry_ref)

    neg = -jnp.inf
    logits = lax.dot_general(wr_ref[...], f_ref[...], (((1,), (1,)), ((), ())),
                             preferred_element_type=F32, precision=HIGHEST)
    scores = jax.nn.sigmoid(logits)
    biased = scores + br_ref[...]

    iota_m = lax.broadcasted_iota(jnp.int32, (GROUP_SIZE, TT), 0).astype(F32)
    groups = [biased[g * GROUP_SIZE:(g + 1) * GROUP_SIZE, :] for g in range(N_GROUPS)]
    gs_rows = []
    for blk in groups:
        m1 = jnp.max(blk, axis=0, keepdims=True)
        i1 = _first_index(blk == m1, iota_m, GROUP_SIZE)
        m2 = jnp.max(jnp.where(iota_m == i1, neg, blk), axis=0, keepdims=True)
        gs_rows.append(m1 + m2)
    gs = jnp.concatenate(gs_rows, axis=0)

    iota_g = lax.broadcasted_iota(jnp.int32, gs.shape, 0).astype(F32)
    sel = jnp.zeros(gs.shape, F32)
    cur = gs
    for _ in range(TOPK_GROUPS):
        m = jnp.max(cur, axis=0, keepdims=True)
        hit = iota_g == _first_index(cur == m, iota_g, N_GROUPS)
        sel = jnp.where(hit, 1.0, sel)
        cur = jnp.where(hit, neg, cur)

    cur = jnp.concatenate(
        [jnp.where(sel[g:g + 1, :] > 0.5, groups[g], neg) for g in range(N_GROUPS)], axis=0)
    iota_e = lax.broadcasted_iota(jnp.int32, cur.shape, 0).astype(F32)
    assigned = jnp.zeros(cur.shape, F32)
    w_rows, hits = [], []
    for k in range(TOP_K):
        m = jnp.max(cur, axis=0, keepdims=True)
        first = _first_index(cur == m, iota_e, N_EXPERTS)
        hit = iota_e == first
        idx_ref[k:k + 1, :] = first.astype(jnp.int32)
        w_rows.append(jnp.sum(jnp.where(hit, scores, 0.0), axis=0, keepdims=True))
        hits.append(hit)
        assigned = jnp.where(hit, 1.0, assigned)
        cur = jnp.where(hit, neg, cur)

    w_sum = w_rows[0]
    for k in range(1, TOP_K):
        w_sum = w_sum + w_rows[k]
    for k in range(TOP_K):
        wt_ref[k:k + 1, :] = w_rows[k] / w_sum * ROUTED_SCALE

    r_i = lax.broadcasted_iota(jnp.int32, (TT, TT), 0)
    c_i = lax.broadcasted_iota(jnp.int32, (TT, TT), 1)
    upper = jnp.where(r_i <= c_i, 1.0, 0.0).astype(BF16)
    incl = jnp.dot(assigned.astype(BF16), upper, preferred_element_type=F32)
    rank_e = carry_ref[...] + incl - assigned
    for k in range(TOP_K):
        rank_k = jnp.sum(jnp.where(hits[k], rank_e, 0.0), axis=0, keepdims=True)
        rank_ref[k:k + 1, :] = rank_k.astype(jnp.int32)
    carry = carry_ref[...] + jnp.sum(assigned, axis=1, keepdims=True)
    carry_ref[...] = carry
    cnt_ref[...] = carry.astype(jnp.int32)


def router(f, w_router, b_router):
    tok = pl.BlockSpec((TOP_K, TT), lambda i: (0, i))
    return pl.pallas_call(
        _router_kernel,
        grid=(N // TT,),
        in_specs=[
            pl.BlockSpec((TT, D), lambda i: (i, 0)),
            pl.BlockSpec((N_EXPERTS, D), lambda i: (0, 0)),
            pl.BlockSpec((N_EXPERTS, 1), lambda i: (0, 0)),
        ],
        out_specs=[tok, tok, tok, pl.BlockSpec((N_EXPERTS, 1), lambda i: (0, 0))],
        out_shape=[
            jax.ShapeDtypeStruct((TOP_K, N), jnp.int32),
            jax.ShapeDtypeStruct((TOP_K, N), F32),
            jax.ShapeDtypeStruct((TOP_K, N), jnp.int32),
            jax.ShapeDtypeStruct((N_EXPERTS, 1), jnp.int32),
        ],
        scratch_shapes=[pltpu.VMEM((N_EXPERTS, 1), F32)],
        compiler_params=_params(("arbitrary",)),
        name="router",
    )(f, w_router.T, b_router.reshape(N_EXPERTS, 1))


def _swiglu(x, wg, wu, wd):
    g = jnp.dot(x, wg, preferred_element_type=F32)
    u = jnp.dot(x, wu, preferred_element_type=F32)
    mid = (g * jax.nn.sigmoid(g) * u).astype(BF16)
    return jnp.dot(mid, wd, preferred_element_type=F32)


def _shared_kernel(x_ref, wg_ref, wu_ref, wd_ref, y_ref, wgb, wub, wdb):
    @pl.when(pl.program_id(0) == 0)
    def _():
        wgb[...] = wg_ref[...].astype(BF16)
        wub[...] = wu_ref[...].astype(BF16)
        wdb[...] = wd_ref[...].astype(BF16)

    y_ref[...] = _swiglu(x_ref[...].astype(BF16), wgb[...], wub[...], wdb[...])


def shared_expert(f, wg, wu, wd, layer):
    dh = wg.shape[2]
    return pl.pallas_call(
        _shared_kernel,
        grid=(N // TT,),
        in_specs=[
            pl.BlockSpec((TT, D), lambda i: (i, 0)),
            pl.BlockSpec((None, D, dh), lambda i: (layer, 0, 0)),
            pl.BlockSpec((None, D, dh), lambda i: (layer, 0, 0)),
            pl.BlockSpec((None, dh, D), lambda i: (layer, 0, 0)),
        ],
        out_specs=pl.BlockSpec((TT, D), lambda i: (i, 0)),
        out_shape=jax.ShapeDtypeStruct((N, D), F32),
        scratch_shapes=[pltpu.VMEM((D, dh), BF16), pltpu.VMEM((D, dh), BF16), pltpu.VMEM((dh, D), BF16)],
        compiler_params=_params(("arbitrary",)),
        name="shared_expert",
    )(f, wg, wu, wd)


def _packed_rows(ref, first_row, n_rows):
    return ref.at[pl.ds(pl.multiple_of(first_row * ROW_TILES, ROW_TILES), n_rows * ROW_TILES)]


def _issue_rows(src_hbm, idx_smem, slot, idx_off, dst, dst_row, sem, n_rows, unroll):
    def issue(r, carry):
        t = idx_smem[slot, 0, idx_off + r]
        pltpu.make_async_copy(_packed_rows(src_hbm, t, 1), _packed_rows(dst, dst_row + r, 1), sem).start()
        return carry

    lax.fori_loop(0, n_rows, issue, 0, unroll=unroll)


def _wait_rows(src_hbm, dst, dst_row, sem, n_rows):
    pltpu.make_async_copy(_packed_rows(src_hbm, 0, n_rows), _packed_rows(dst, dst_row, n_rows), sem).wait()


def _routed_kernel(be_ref, nu_ref, tok_hbm, x_hbm, wg_ref, wu_ref, wd_ref, y_ref,
                   idx_smem, xbuf, wgb, wub, wdb, sem_i, sem_g):
    j = pl.program_id(0)
    n_used = nu_ref[0]
    last_blk = pl.num_programs(0) - 1
    slot = j % 2
    nxt = 1 - slot

    def idx_copy(blk, s):
        return pltpu.make_async_copy(tok_hbm.at[jnp.minimum(blk, last_blk)], idx_smem.at[s], sem_i.at[s])

    def rows_issue(s):
        _issue_rows(x_hbm, idx_smem, s, 0, xbuf, s * MOE_BLOCK, sem_g.at[s], MOE_BLOCK, 8)

    def rows_wait(s):
        _wait_rows(x_hbm, xbuf, s * MOE_BLOCK, sem_g.at[s], MOE_BLOCK)

    @pl.when(jnp.logical_and(j == 0, n_used > 0))
    def _():
        first = idx_copy(0, 0)
        first.start()
        first.wait()
        rows_issue(0)
        idx_copy(1, 1).start()

    @pl.when(j >= n_used)
    def _():
        y_ref[...] = jnp.zeros_like(y_ref)

    @pl.when(j < n_used)
    def _():
        changed = jnp.logical_or(j == 0, be_ref[j] != be_ref[jnp.maximum(j - 1, 0)])

        @pl.when(changed)
        def _():
            wgb[...] = wg_ref[...].astype(BF16)
            wub[...] = wu_ref[...].astype(BF16)
            wdb[...] = wd_ref[...].astype(BF16)

        idx_copy(j + 1, nxt).wait()
        rows_issue(nxt)
        idx_copy(j + 2, slot).start()

        rows_wait(slot)
        pieces = [_load_packed(xbuf, slot * MOE_BLOCK, MOE_BLOCK, s) for s in range(ROW_TILES)]
        x = jnp.concatenate([p[0].astype(BF16) for p in pieces] + [p[1].astype(BF16) for p in pieces], axis=1)
        _store_packed(y_ref, _swiglu(x, wgb[...], wub[...], wdb[...]))

        @pl.when(j == n_used - 1)
        def _():
            rows_wait(nxt)
            idx_copy(j + 2, slot).wait()


def routed_experts(x_packed, tok_buf, block_expert, n_used, wg, wu, wd, layer):
    n_blocks = block_expert.shape[0]
    dh = wg.shape[3]
    grid_spec = pltpu.PrefetchScalarGridSpec(
        num_scalar_prefetch=2,
        grid=(n_blocks,),
        in_specs=[
            pl.BlockSpec(memory_space=pl.ANY),
            pl.BlockSpec(memory_space=pl.ANY),
            pl.BlockSpec((None, None, D, dh), lambda j, be, nu: (layer, be[j], 0, 0)),
            pl.BlockSpec((None, None, D, dh), lambda j, be, nu: (layer, be[j], 0, 0)),
            pl.BlockSpec((None, None, dh, D), lambda j, be, nu: (layer, be[j], 0, 0)),
        ],
        out_specs=pl.BlockSpec((MOE_BLOCK * ROW_TILES, LANES), lambda j, be, nu: (j, 0)),
        scratch_shapes=[
            pltpu.SMEM((2, 1, MOE_BLOCK), jnp.int32),
            pltpu.VMEM((2 * MOE_BLOCK * ROW_TILES, LANES), jnp.uint32),
            pltpu.VMEM((D, dh), BF16), pltpu.VMEM((D, dh), BF16), pltpu.VMEM((dh, D), BF16),
            pltpu.SemaphoreType.DMA((2,)), pltpu.SemaphoreType.DMA((2,)),
        ],
    )
    return pl.pallas_call(
        _routed_kernel,
        grid_spec=grid_spec,
        out_shape=jax.ShapeDtypeStruct((n_blocks * MOE_BLOCK * ROW_TILES, LANES), jnp.uint32),
        compiler_params=_params(("arbitrary",)),
        name="routed_experts",
    )(block_expert, n_used, tok_buf, x_packed, wg, wu, wd)


COMBINE_T = 128


def _combine_kernel(dest_hbm, ys_hbm, wt_ref, sh_ref, h_ref, gate_ref, o_ref, idx_smem, gbuf, sem_i, sem_g):
    i = pl.program_id(0)
    n_tiles = pl.num_programs(0)
    slot = i % 2
    nxt = 1 - slot

    def idx_copy(tile, s):
        return pltpu.make_async_copy(dest_hbm.at[tile], idx_smem.at[s], sem_i.at[s])

    def issue_tile(s):
        _issue_rows(ys_hbm, idx_smem, s, 0, gbuf, s * TOP_K * COMBINE_T, sem_g.at[s], TOP_K * COMBINE_T, 8)

    @pl.when(i == 0)
    def _():
        first = idx_copy(0, 0)
        first.start()
        first.wait()
        issue_tile(0)
        idx_copy(1, 1).start()

    @pl.when(i + 1 < n_tiles)
    def _():
        idx_copy(i + 1, nxt).wait()
        issue_tile(nxt)

    @pl.when(i + 2 < n_tiles)
    def _():
        idx_copy(i + 2, slot).start()

    _wait_rows(ys_hbm, gbuf, slot * TOP_K * COMBINE_T, sem_g.at[slot], TOP_K * COMBINE_T)
    wts = [wt_ref[:, k:k + 1] for k in range(TOP_K)]
    for s in range(ROW_TILES):
        c_lo = slice(s * LANES, (s + 1) * LANES)
        c_hi = slice(HALF + s * LANES, HALF + (s + 1) * LANES)
        acc_lo = sh_ref[:, c_lo]
        acc_hi = sh_ref[:, c_hi]
        for k in range(TOP_K):
            lo, hi = _load_packed(gbuf, (slot * TOP_K + k) * COMBINE_T, COMBINE_T, s)
            acc_lo = acc_lo + wts[k] * lo
            acc_hi = acc_hi + wts[k] * hi
        o_ref[:, c_lo] = h_ref[:, c_lo] + gate_ref[:, c_lo] * acc_lo
        o_ref[:, c_hi] = h_ref[:, c_hi] + gate_ref[:, c_hi] * acc_hi


def combine(dest, ys, wts, shared, h, mod3, gate_idx):
    per_tt = TT // COMBINE_T
    return pl.pallas_call(
        _combine_kernel,
        grid=(N // COMBINE_T,),
        in_specs=[
            pl.BlockSpec(memory_space=pl.ANY),
            pl.BlockSpec(memory_space=pl.ANY),
            pl.BlockSpec((COMBINE_T, TOP_K), lambda i: (i, 0)),
            pl.BlockSpec((COMBINE_T, D), lambda i: (i, 0)),
            pl.BlockSpec((COMBINE_T, D), lambda i: (i, 0)),
            pl.BlockSpec((None, 1, D), lambda i: (_mod_row(i // per_tt) * N_MOD + gate_idx, 0, 0)),
        ],
        out_specs=pl.BlockSpec((COMBINE_T, D), lambda i: (i, 0)),
        out_shape=jax.ShapeDtypeStruct((N, D), F32),
        scratch_shapes=[
            pltpu.SMEM((2, 1, TOP_K * COMBINE_T), jnp.int32),
            pltpu.VMEM((2 * TOP_K * COMBINE_T * ROW_TILES, LANES), jnp.uint32),
            pltpu.SemaphoreType.DMA((2,)), pltpu.SemaphoreType.DMA((2,)),
        ],
        compiler_params=_params(("arbitrary",)),
        name="combine",
    )(dest, ys, wts, shared, h, mod3)


def moe_layer(h, f, f_packed, mod3, gate_idx, layer, w_router, b_router, wg, wu, wd, wsg, wsu, wsd):
    idx, wts, rank, counts = router(f, w_router, b_router)
    counts = counts[:, 0]
    padded = (counts + MOE_BLOCK - 1) // MOE_BLOCK * MOE_BLOCK
    pad_end = jnp.cumsum(padded)
    pad_start = pad_end - padded
    onehot = idx[:, :, None] == jnp.arange(N_EXPERTS, dtype=jnp.int32)[None, None, :]
    dest = jnp.sum(jnp.where(onehot, pad_start[None, None, :], 0), axis=-1) + rank
    n_blocks = -(-(N * TOP_K + N_EXPERTS * (MOE_BLOCK - 1)) // MOE_BLOCK)
    cap = n_blocks * MOE_BLOCK
    tok = jnp.broadcast_to(jnp.arange(N, dtype=jnp.int32)[None, :], dest.shape)
    filler = jnp.arange(cap, dtype=jnp.int32) % N
    tok_buf = filler.at[dest.reshape(-1)].set(tok.reshape(-1))
    block_start = jnp.arange(n_blocks, dtype=jnp.int32) * MOE_BLOCK
    block_expert = jnp.sum(block_start[:, None] >= pad_end[None, :], axis=1)
    block_expert = jnp.minimum(block_expert, N_EXPERTS - 1).astype(jnp.int32)
    n_used = (pad_end[-1:] // MOE_BLOCK).astype(jnp.int32)
    ys = routed_experts(f_packed, tok_buf.reshape(n_blocks, 1, MOE_BLOCK), block_expert, n_used,
                        wg, wu, wd, layer)
    shared = shared_expert(f, wsg, wsu, wsd, layer)
    dest_tiles = dest.reshape(TOP_K, N // COMBINE_T, COMBINE_T).transpose(1, 0, 2)
    dest_tiles = dest_tiles.reshape(N // COMBINE_T, 1, TOP_K * COMBINE_T)
    return combine(dest_tiles, ys, wts.T, shared, h, mod3, gate_idx)


def _rope_tables(rot_dim):
    rows = T // GRID_W
    row = np.repeat(np.arange(rows, dtype=np.float32), GRID_W)
    col = np.tile(np.arange(GRID_W, dtype=np.float32), rows)
    half = rot_dim // 2
    inv_freq = jnp.asarray(ROPE_THETA, F32) ** (-jnp.arange(0, half, 2, dtype=F32) / half)
    ang_r = jnp.asarray(row)[:, None] * inv_freq[None, :]
    ang_c = jnp.asarray(col)[:, None] * inv_freq[None, :]
    ang = jnp.concatenate([ang_r, ang_r, ang_c, ang_c], axis=-1)
    cos, sin = jnp.cos(ang), jnp.sin(ang)
    quarter = (np.arange(rot_dim) // (rot_dim // 4)) % 2
    s1 = jnp.where(quarter[None, :] == 0, -sin, 0.0)
    s2 = jnp.where(quarter[None, :] == 1, sin, 0.0)

    def full(tbl, fill):
        tbl = jnp.pad(tbl, ((0, 0), (0, LANES - rot_dim)), constant_values=fill)
        return jnp.concatenate([jnp.full((CTX, LANES), fill, F32), tbl], axis=0)

    return full(cos, 1.0), full(s1, 0.0), full(s2, 0.0)


def _pad_heads(w, n_heads, width, new_width):
    k = w.shape[0]
    w = w.reshape(k, n_heads, width)
    return jnp.pad(w, ((0, 0), (0, 0), (0, new_width - width))).reshape(k, n_heads * new_width)


def _pad_gain(g, new_width):
    return jnp.pad(g, (0, new_width - g.shape[0])).reshape(1, new_width)


def mla_mixer(a, tables, w_down, g_q, g_kv, w_uq, w_ukv, g_qn, g_kn, w_o_args):
    c, s1, s2 = tables
    w_dq = w_down[:, :MLA_Q_RANK]
    w_dkv = w_down[:, MLA_Q_RANK:MLA_Q_RANK + MLA_KV_RANK]
    w_dr = jnp.pad(w_down[:, MLA_Q_RANK + MLA_KV_RANK:], ((0, 0), (0, LANES - MLA_ROPE)))
    cq = matmul(a, w_dq, tn=MLA_Q_RANK, extra=(g_q.reshape(1, -1),), extra_specs=(_const_spec(MLA_Q_RANK),),
                out_shapes=[jax.ShapeDtypeStruct((N, MLA_Q_RANK), BF16)], out_specs=[_row_spec(MLA_Q_RANK)],
                epilogue=_epi_rmsnorm, name="mla_down_q")[0]
    ckv = matmul(a, w_dkv, tn=MLA_KV_RANK, extra=(g_kv.reshape(1, -1),),
                 extra_specs=(_const_spec(MLA_KV_RANK),),
                 out_shapes=[jax.ShapeDtypeStruct((N, MLA_KV_RANK), BF16)], out_specs=[_row_spec(MLA_KV_RANK)],
                 epilogue=_epi_rmsnorm, name="mla_down_kv")[0]
    k_rope = matmul(a, w_dr, tn=LANES, out_shapes=[jax.ShapeDtypeStruct((N, LANES), F32)],
                    out_specs=[_row_spec(LANES)], epilogue=_epi_plain, name="mla_down_rope")[0]

    tn = 1024
    heads_t = tn // MLA_HW
    scale = MLA_QK ** -0.5 * LOG2E
    w_uq_p = _pad_heads(w_uq, MLA_HEADS, MLA_QK, MLA_HW)
    q = matmul(cq, w_uq_p, tn=tn, extra=(_pad_gain(g_qn * scale, MLA_HW), c, s1, s2),
               extra_specs=(_const_spec(MLA_HW), _table_spec(), _table_spec(), _table_spec()),
               out_shapes=[jax.ShapeDtypeStruct((N, MLA_HEADS * MLA_HW), BF16)], out_specs=[_row_spec(tn)],
               epilogue=functools.partial(_epi_mla_q, n_heads=heads_t), name="mla_up_q")[0]
    k, v = matmul(ckv, w_ukv, tn=tn, extra=(k_rope, _pad_gain(g_kn, MLA_HW), c, s1, s2),
                  extra_specs=(pl.BlockSpec((TT, LANES), lambda n, i: (i, 0)), _const_spec(MLA_HW),
                               _table_spec(), _table_spec(), _table_spec()),
                  out_shapes=[jax.ShapeDtypeStruct((N, MLA_HEADS * MLA_HW), BF16),
                              jax.ShapeDtypeStruct((N, MLA_HEADS * MLA_V), BF16)],
                  out_specs=[_row_spec(tn), _row_spec(tn // 2)],
                  epilogue=functools.partial(_epi_mla_kv, n_heads=heads_t), name="mla_up_kv")
    o = attention(q, k, v, n_groups=MLA_HEADS, n_rep=1, dqk=MLA_HW, dv=MLA_V, with_ctx=True)
    return _out_proj(o, *w_o_args)


def gqa_mixer(a, tables, w_qkv, g_qn, g_kn, w_o_args):
    c, s1, s2 = tables
    tn = 512
    heads_t = tn // GQA_HD
    scale = GQA_HD ** -0.5 * LOG2E
    rope_specs = (_const_spec(GQA_HD), _table_spec(), _table_spec(), _table_spec())
    nq = GQA_HEADS * GQA_HD
    nk = GQA_KV_HEADS * GQA_HD
    q = matmul(a, w_qkv, tn=tn, n_cols=nq, extra=((g_qn * scale).reshape(1, -1), c, s1, s2),
               extra_specs=rope_specs, out_shapes=[jax.ShapeDtypeStruct((N, nq), BF16)],
               out_specs=[_row_spec(tn)], epilogue=functools.partial(_epi_head_rope, n_heads=heads_t),
               name="gqa_q")[0]
    k = matmul(a, w_qkv, tn=tn, col_off=nq // tn, n_cols=nk, extra=(g_kn.reshape(1, -1), c, s1, s2),
               extra_specs=rope_specs, out_shapes=[jax.ShapeDtypeStruct((N, nk), BF16)],
               out_specs=[_row_spec(tn)], epilogue=functools.partial(_epi_head_rope, n_heads=heads_t),
               name="gqa_k")[0]
    v = matmul(a, w_qkv, tn=tn, col_off=(nq + nk) // tn, n_cols=nk,
               out_shapes=[jax.ShapeDtypeStruct((N, nk), BF16)], out_specs=[_row_spec(tn)],
               epilogue=_epi_plain, name="gqa_v")[0]
    o = attention(q, k, v, n_groups=GQA_KV_HEADS, n_rep=GQA_REP, dqk=GQA_HD, dv=GQA_HD, with_ctx=True)
    return _out_proj(o, *w_o_args)


def _out_proj(o, w_o, h, mod3, gate_idx):
    tn = 1024
    return matmul(o, w_o, tn=tn, extra=(h, mod3),
                  extra_specs=(_row_spec(tn),
                               pl.BlockSpec((None, 1, tn), lambda n, i: (_mod_row(i) * N_MOD + gate_idx, 0, n))),
                  out_shapes=[jax.ShapeDtypeStruct((N, D), F32)], out_specs=[_row_spec(tn)],
                  epilogue=_epi_resgate, name="out_proj")[0]


def kernel(x, c, ctx, c_ctx, ada_w, ada_b, norm1_g, norm2_g, mla_w_down, mla_g_q, mla_g_kv, mla_w_uq,
           mla_w_ukv, mla_g_qn, mla_g_kn, mla_w_o, gqa_w_qkv, gqa_g_qn, gqa_g_kn, gqa_w_o, moe_w_router,
           moe_b_router, moe_w_gate, moe_w_up, moe_w_down, moe_ws_gate, moe_ws_up, moe_ws_down):
    depth = ada_w.shape[0]
    h = jnp.concatenate([ctx, x], axis=1).reshape(N, D)
    c_rows = jnp.concatenate([c_ctx[None, :], c, jnp.zeros((16 - 1 - B, D), F32)], axis=0)
    mod = ada_table(c_rows, ada_w, ada_b)
    tables_a = _rope_tables(MLA_ROPE)
    tables_b = _rope_tables(GQA_HD)
    for i in range(depth):
        mod3 = mod[i].reshape(16 * N_MOD, 1, D)
        a = norm_mod(h, norm1_g[i], mod3, 0, 1, BF16)
        j = i // 2
        if i % 2 == 0:
            h = mla_mixer(a, tables_a, mla_w_down[j], mla_g_q[j], mla_g_kv[j], mla_w_uq[j], mla_w_ukv[j],
                          mla_g_qn[j], mla_g_kn[j], (mla_w_o[j], h, mod3, 2))
        else:
            h = gqa_mixer(a, tables_b, gqa_w_qkv[j], gqa_g_qn[j], gqa_g_kn[j], (gqa_w_o[j], h, mod3, 2))
        f, f_packed = norm_mod(h, norm2_g[i], mod3, 3, 4, F32, with_packed=True)
        h = moe_layer(h, f, f_packed, mod3, 5, i, moe_w_router[i], moe_b_router[i], moe_w_gate, moe_w_up,
                      moe_w_down, moe_ws_gate, moe_ws_up, moe_ws_down)
    return h.reshape(B, S, D)[:, CTX:, :]
```

```python
import functools

import jax
import jax.numpy as jnp
import numpy as np
from jax import lax
from jax.experimental import pallas as pl
from jax.experimental.pallas import tpu as pltpu

F32 = jnp.float32
BF16 = jnp.bfloat16
HIGHEST = lax.Precision.HIGHEST

D = 2048
B = 8
T = 2048
CTX = 256
S = CTX + T
N = B * S
TT = 256
TILES_B = S // TT
GRID_W = 64
ROPE_THETA = 10000.0
EPS = 1e-6
N_MOD = 6

MLA_HEADS = 16
MLA_Q_RANK = 768
MLA_KV_RANK = 512
MLA_NOPE = 128
MLA_ROPE = 64
MLA_QK = MLA_NOPE + MLA_ROPE
MLA_V = 128
MLA_HW = 256

GQA_HEADS = 16
GQA_KV_HEADS = 4
GQA_REP = GQA_HEADS // GQA_KV_HEADS
GQA_HD = 128

N_EXPERTS = 64
N_GROUPS = 8
GROUP_SIZE = N_EXPERTS // N_GROUPS
TOPK_GROUPS = 4
TOP_K = 8
D_EXPERT = 512
ROUTED_SCALE = 2.5
MOE_BLOCK = 256

LANES = 128
LOG2E = 1.4426950408889634
VMEM_LIMIT = 56 * 1024 * 1024


def _params(sem, vmem=VMEM_LIMIT):
    return pltpu.CompilerParams(dimension_semantics=sem, vmem_limit_bytes=vmem)


def _mod_row(i):
    return jnp.where(i % TILES_B == 0, 0, i // TILES_B + 1)


def _ada_kernel(c_ref, w_ref, b_ref, o_ref):
    a = c_ref[...]
    a = a * jax.nn.sigmoid(a)
    o_ref[...] = jnp.dot(a, w_ref[...], preferred_element_type=F32, precision=HIGHEST) + b_ref[...]


def ada_table(c_rows, ada_w, ada_b):
    depth = ada_w.shape[0]
    tn = 1024
    return pl.pallas_call(
        _ada_kernel,
        grid=(depth, N_MOD * D // tn),
        in_specs=[
            pl.BlockSpec((16, D), lambda l, n: (0, 0)),
            pl.BlockSpec((None, D, tn), lambda l, n: (l, 0, n)),
            pl.BlockSpec((None, 1, tn), lambda l, n: (l, 0, n)),
        ],
        out_specs=pl.BlockSpec((None, 16, tn), lambda l, n: (l, 0, n)),
        out_shape=jax.ShapeDtypeStruct((depth, 16, N_MOD * D), F32),
        compiler_params=_params(("parallel", "parallel")),
        name="ada_table",
    )(c_rows, ada_w, ada_b.reshape(depth, 1, N_MOD * D))


HALF = D // 2
ROW_TILES = HALF // LANES


def _store_packed(ref, y):
    rows = y.shape[0]
    for s in range(ROW_TILES):
        lo = y[:, s * LANES:(s + 1) * LANES]
        hi = y[:, HALF + s * LANES:HALF + (s + 1) * LANES]
        ref[pl.ds(s, rows, stride=ROW_TILES), :] = pltpu.pack_elementwise([lo, hi], packed_dtype=BF16)


def _load_packed(ref, base, rows, s):
    u = ref[pl.ds(base * ROW_TILES + s, rows, stride=ROW_TILES), :]
    lo = pltpu.unpack_elementwise(u, index=0, packed_dtype=BF16, unpacked_dtype=F32)
    hi = pltpu.unpack_elementwise(u, index=1, packed_dtype=BF16, unpacked_dtype=F32)
    return lo, hi


def _norm_mod_kernel(h_ref, g_ref, sh_ref, sc_ref, o_ref, *packed_ref):
    x = h_ref[...]
    ms = jnp.mean(x * x, axis=-1, keepdims=True)
    y = x * lax.rsqrt(ms + EPS) * g_ref[...]
    y = y * (1.0 + sc_ref[...]) + sh_ref[...]
    o_ref[...] = y.astype(o_ref.dtype)
    if packed_ref:
        _store_packed(packed_ref[0], y)


def norm_mod(h, gain, mod3, shift_idx, scale_idx, out_dtype, with_packed=False):
    row = pl.BlockSpec((TT, D), lambda i: (i, 0))
    out_specs, out_shape = [row], [jax.ShapeDtypeStruct((N, D), out_dtype)]
    if with_packed:
        out_specs.append(pl.BlockSpec((TT * ROW_TILES, LANES), lambda i: (i, 0)))
        out_shape.append(jax.ShapeDtypeStruct((N * ROW_TILES, LANES), jnp.uint32))
    outs = pl.pallas_call(
        _norm_mod_kernel,
        grid=(N // TT,),
        in_specs=[
            row,
            pl.BlockSpec((1, D), lambda i: (0, 0)),
            pl.BlockSpec((None, 1, D), lambda i: (_mod_row(i) * N_MOD + shift_idx, 0, 0)),
            pl.BlockSpec((None, 1, D), lambda i: (_mod_row(i) * N_MOD + scale_idx, 0, 0)),
        ],
        out_specs=out_specs,
        out_shape=out_shape,
        compiler_params=_params(("parallel",)),
        name="norm_mod",
    )(h, gain.reshape(1, D), mod3, mod3)
    return outs if with_packed else outs[0]


def _mm_kernel(*refs, n_extra, n_out, epilogue):
    a_ref, w_ref = refs[0], refs[1]
    extra = refs[2:2 + n_extra]
    outs = refs[2 + n_extra:2 + n_extra + n_out]
    wb_ref = refs[2 + n_extra + n_out]

    @pl.when(pl.program_id(1) == 0)
    def _():
        wb_ref[...] = w_ref[...].astype(BF16)

    acc = jnp.dot(a_ref[...].astype(BF16), wb_ref[...], preferred_element_type=F32)
    epilogue(acc, extra, outs)


def matmul(a, w, *, tn, col_off=0, n_cols=None, extra=(), extra_specs=(), out_shapes, out_specs, epilogue,
           name):
    k = a.shape[1]
    n_cols = w.shape[1] if n_cols is None else n_cols
    kern = functools.partial(_mm_kernel, n_extra=len(extra), n_out=len(out_shapes), epilogue=epilogue)
    return pl.pallas_call(
        kern,
        grid=(n_cols // tn, N // TT),
        in_specs=[
            pl.BlockSpec((TT, k), lambda n, i: (i, 0)),
            pl.BlockSpec((k, tn), lambda n, i: (0, n + col_off)),
            *extra_specs,
        ],
        out_specs=out_specs,
        out_shape=out_shapes,
        scratch_shapes=[pltpu.VMEM((k, tn), BF16)],
        compiler_params=_params(("parallel", "arbitrary")),
        name=name,
    )(a, w, *extra)


def _epi_plain(acc, extra, outs):
    outs[0][...] = acc.astype(outs[0].dtype)


def _epi_rmsnorm(acc, extra, outs):
    (g_ref,) = extra
    ms = jnp.mean(acc * acc, axis=-1, keepdims=True)
    outs[0][...] = (acc * lax.rsqrt(ms + EPS) * g_ref[...]).astype(outs[0].dtype)


def _epi_resgate(acc, extra, outs):
    res_ref, gate_ref = extra
    outs[0][...] = res_ref[...] + gate_ref[...] * acc


def _rope(x, c, s1, s2, quarter):
    return x * c + pltpu.roll(x, LANES - quarter, 1) * s1 + pltpu.roll(x, quarter, 1) * s2


def _epi_head_rope(acc, extra, outs, *, n_heads):
    g_ref, c_ref, s1_ref, s2_ref = extra
    c, s1, s2 = c_ref[...], s1_ref[...], s2_ref[...]
    for j in range(n_heads):
        x = acc[:, j * LANES:(j + 1) * LANES]
        ms = jnp.mean(x * x, axis=-1, keepdims=True)
        xn = x * lax.rsqrt(ms + EPS) * g_ref[...]
        outs[0][:, j * LANES:(j + 1) * LANES] = _rope(xn, c, s1, s2, GQA_HD // 4).astype(outs[0].dtype)


def _mla_head(nope, rope, g_ref, c, s1, s2):
    ss = jnp.sum(nope * nope, axis=-1, keepdims=True) + jnp.sum(rope * rope, axis=-1, keepdims=True)
    r = lax.rsqrt(ss * (1.0 / MLA_QK) + EPS)
    nope_n = nope * r * g_ref[:, :LANES]
    rope_n = _rope(rope * r * g_ref[:, LANES:], c, s1, s2, MLA_ROPE // 4)
    return nope_n, rope_n


def _epi_mla_q(acc, extra, outs, *, n_heads):
    g_ref, c_ref, s1_ref, s2_ref = extra
    c, s1, s2 = c_ref[...], s1_ref[...], s2_ref[...]
    for j in range(n_heads):
        nope = acc[:, j * MLA_HW:j * MLA_HW + LANES]
        rope = acc[:, j * MLA_HW + LANES:(j + 1) * MLA_HW]
        nope_n, rope_n = _mla_head(nope, rope, g_ref, c, s1, s2)
        outs[0][:, j * MLA_HW:j * MLA_HW + LANES] = nope_n.astype(outs[0].dtype)
        outs[0][:, j * MLA_HW + LANES:(j + 1) * MLA_HW] = rope_n.astype(outs[0].dtype)


def _epi_mla_kv(acc, extra, outs, *, n_heads):
    kr_ref, g_ref, c_ref, s1_ref, s2_ref = extra
    k_out, v_out = outs
    c, s1, s2 = c_ref[...], s1_ref[...], s2_ref[...]
    rope = kr_ref[...]
    for j in range(n_heads):
        nope = acc[:, j * MLA_HW:j * MLA_HW + LANES]
        v = acc[:, j * MLA_HW + LANES:(j + 1) * MLA_HW]
        nope_n, rope_n = _mla_head(nope, rope, g_ref, c, s1, s2)
        k_out[:, j * MLA_HW:j * MLA_HW + LANES] = nope_n.astype(k_out.dtype)
        k_out[:, j * MLA_HW + LANES:(j + 1) * MLA_HW] = rope_n.astype(k_out.dtype)
        v_out[:, j * LANES:(j + 1) * LANES] = v.astype(v_out.dtype)


def _row_spec(width):
    return pl.BlockSpec((TT, width), lambda n, i: (i, n))


def _const_spec(width):
    return pl.BlockSpec((1, width), lambda n, i: (0, 0))


def _table_spec():
    return pl.BlockSpec((TT, LANES), lambda n, i: (i % TILES_B, 0))


KEY_CHUNK = 768


def _attn_kernel(q_ref, k_ref, v_ref, o_ref, *, n_heads, kv_shared, dqk, dv, with_ctx):
    def compute(nk):
        chunk = min(KEY_CHUNK, nk)
        n_chunks = nk // chunk

        def scores(h, c):
            kh = 0 if kv_shared else h
            q = q_ref[:, h * dqk:(h + 1) * dqk]
            k = k_ref[c * chunk:(c + 1) * chunk, kh * dqk:(kh + 1) * dqk]
            return lax.dot_general(q, k, (((1,), (1,)), ((), ())), preferred_element_type=F32)

        def row_max(s_chunks):
            m = jnp.max(s_chunks[0], axis=-1, keepdims=True)
            for s in s_chunks[1:]:
                m = jnp.maximum(m, jnp.max(s, axis=-1, keepdims=True))
            return m

        s_cur = [scores(0, c) for c in range(n_chunks)]
        for h in range(n_heads):
            vh = 0 if kv_shared else h
            m = row_max(s_cur)
            s_next = []
            l = None
            o = None
            for c in range(n_chunks):
                p = jnp.exp2(s_cur[c] - m)
                l_c = jnp.sum(p, axis=-1, keepdims=True)
                v = v_ref[c * chunk:(c + 1) * chunk, vh * dv:(vh + 1) * dv]
                o_c = jnp.dot(p.astype(BF16), v, preferred_element_type=F32)
                l = l_c if l is None else l + l_c
                o = o_c if o is None else o + o_c
                if h + 1 < n_heads:
                    s_next.append(scores(h + 1, c))
            o_ref[:, h * dv:(h + 1) * dv] = (o / l).astype(o_ref.dtype)
            s_cur = s_next

    if with_ctx:
        @pl.when(pl.program_id(2) == 0)
        def _():
            compute(CTX)

        @pl.when(pl.program_id(2) > 0)
        def _():
            compute(S)
    else:
        compute(S)


def attention(q, k, v, *, n_steps, n_heads, kv_shared, dqk, dv, with_ctx):
    first = 0 if with_ctx else 1
    kv_heads = 1 if kv_shared else n_heads
    kern = functools.partial(_attn_kernel, n_heads=n_heads, kv_shared=kv_shared, dqk=dqk, dv=dv,
                             with_ctx=with_ctx)
    return pl.pallas_call(
        kern,
        grid=(B, n_steps, TILES_B - first),
        in_specs=[
            pl.BlockSpec((TT, n_heads * dqk), lambda b, g, i: (b * TILES_B + first + i, g)),
            pl.BlockSpec((S, kv_heads * dqk), lambda b, g, i: (b, g)),
            pl.BlockSpec((S, kv_heads * dv), lambda b, g, i: (b, g)),
        ],
        out_specs=pl.BlockSpec((TT, n_heads * dv), lambda b, g, i: (b * TILES_B + first + i, g)),
        out_shape=jax.ShapeDtypeStruct((N, n_steps * n_heads * dv), BF16),
        compiler_params=_params(("parallel", "parallel", "arbitrary")),
        name="attention",
    )(q, k, v)


def _first_index(hit, iota, size):
    return jnp.min(jnp.where(hit, iota, float(size)), axis=0, keepdims=True)


def _router_kernel(f_ref, wr_ref, br_ref, idx_ref, wt_ref, rank_ref, cnt_ref, carry_ref):
    @pl.when(pl.program_id(0) == 0)
    def _():
        carry_ref[...] = jnp.zeros_like(carry_ref)

    neg = -jnp.inf
    logits = lax.dot_general(wr_ref[...], f_ref[...], (((1,), (1,)), ((), ())),
                             preferred_element_type=F32, precision=HIGHEST)
    scores = jax.nn.sigmoid(logits)
    biased = scores + br_ref[...]

    iota_m = lax.broadcasted_iota(jnp.int32, (GROUP_SIZE, TT), 0).astype(F32)
    groups = [biased[g * GROUP_SIZE:(g + 1) * GROUP_SIZE, :] for g in range(N_GROUPS)]
    gs_rows = []
    for blk in groups:
        m1 = jnp.max(blk, axis=0, keepdims=True)
        i1 = _first_index(blk == m1, iota_m, GROUP_SIZE)
        m2 = jnp.max(jnp.where(iota_m == i1, neg, blk), axis=0, keepdims=True)
        gs_rows.append(m1 + m2)
    gs = jnp.concatenate(gs_rows, axis=0)

    iota_g = lax.broadcasted_iota(jnp.int32, gs.shape, 0).astype(F32)
    sel = jnp.zeros(gs.shape, F32)
    cur = gs
    for _ in range(TOPK_GROUPS):
        m = jnp.max(cur, axis=0, keepdims=True)
        hit = iota_g == _first_index(cur == m, iota_g, N_GROUPS)
        sel = jnp.where(hit, 1.0, sel)
        cur = jnp.where(hit, neg, cur)

    cur = jnp.concatenate(
        [jnp.where(sel[g:g + 1, :] > 0.5, groups[g], neg) for g in range(N_GROUPS)], axis=0)
    iota_e = lax.broadcasted_iota(jnp.int32, cur.shape, 0).astype(F32)
    assigned = jnp.zeros(cur.shape, F32)
    w_rows, hits = [], []
    for k in range(TOP_K):
        m = jnp.max(cur, axis=0, keepdims=True)
        first = _first_index(cur == m, iota_e, N_EXPERTS)
        hit = iota_e == first
        idx_ref[k:k + 1, :] = first.astype(jnp.int32)
        w_rows.append(jnp.sum(jnp.where(hit, scores, 0.0), axis=0, keepdims=True))
        hits.append(hit)
        assigned = jnp.where(hit, 1.0, assigned)
        cur = jnp.where(hit, neg, cur)

    w_sum = w_rows[0]
    for k in range(1, TOP_K):
        w_sum = w_sum + w_rows[k]
    for k in range(TOP_K):
        wt_ref[k:k + 1, :] = w_rows[k] / w_sum * ROUTED_SCALE

    r_i = lax.broadcasted_iota(jnp.int32, (TT, TT), 0)
    c_i = lax.broadcasted_iota(jnp.int32, (TT, TT), 1)
    upper = jnp.where(r_i <= c_i, 1.0, 0.0).astype(BF16)
    incl = jnp.dot(assigned.astype(BF16), upper, preferred_element_type=F32)
    rank_e = carry_ref[...] + incl - assigned
    for k in range(TOP_K):
        rank_k = jnp.sum(jnp.where(hits[k], rank_e, 0.0), axis=0, keepdims=True)
        rank_ref[k:k + 1, :] = rank_k.astype(jnp.int32)
    carry = carry_ref[...] + jnp.sum(assigned, axis=1, keepdims=True)
    carry_ref[...] = carry
    cnt_ref[...] = carry.astype(jnp.int32)


def router(f, w_router, b_router):
    tok = pl.BlockSpec((TOP_K, TT), lambda i: (0, i))
    return pl.pallas_call(
        _router_kernel,
        grid=(N // TT,),
        in_specs=[
            pl.BlockSpec((TT, D), lambda i: (i, 0)),
            pl.BlockSpec((N_EXPERTS, D), lambda i: (0, 0)),
            pl.BlockSpec((N_EXPERTS, 1), lambda i: (0, 0)),
        ],
        out_specs=[tok, tok, tok, pl.BlockSpec((N_EXPERTS, 1), lambda i: (0, 0))],
        out_shape=[
            jax.ShapeDtypeStruct((TOP_K, N), jnp.int32),
            jax.ShapeDtypeStruct((TOP_K, N), F32),
            jax.ShapeDtypeStruct((TOP_K, N), jnp.int32),
            jax.ShapeDtypeStruct((N_EXPERTS, 1), jnp.int32),
        ],
        scratch_shapes=[pltpu.VMEM((N_EXPERTS, 1), F32)],
        compiler_params=_params(("arbitrary",)),
        name="router",
    )(f, w_router.T, b_router.reshape(N_EXPERTS, 1))


def _swiglu(x, wg, wu, wd):
    g = jnp.dot(x, wg, preferred_element_type=F32)
    u = jnp.dot(x, wu, preferred_element_type=F32)
    mid = (g * jax.nn.sigmoid(g) * u).astype(BF16)
    return jnp.dot(mid, wd, preferred_element_type=F32)


def _shared_kernel(x_ref, wg_ref, wu_ref, wd_ref, y_ref, wgb, wub, wdb):
    @pl.when(pl.program_id(0) == 0)
    def _():
        wgb[...] = wg_ref[...].astype(BF16)
        wub[...] = wu_ref[...].astype(BF16)
        wdb[...] = wd_ref[...].astype(BF16)

    y_ref[...] = _swiglu(x_ref[...].astype(BF16), wgb[...], wub[...], wdb[...])


def shared_expert(f, wg, wu, wd, layer):
    dh = wg.shape[2]
    return pl.pallas_call(
        _shared_kernel,
        grid=(N // TT,),
        in_specs=[
            pl.BlockSpec((TT, D), lambda i: (i, 0)),
            pl.BlockSpec((None, D, dh), lambda i: (layer, 0, 0)),
            pl.BlockSpec((None, D, dh), lambda i: (layer, 0, 0)),
            pl.BlockSpec((None, dh, D), lambda i: (layer, 0, 0)),
        ],
        out_specs=pl.BlockSpec((TT, D), lambda i: (i, 0)),
        out_shape=jax.ShapeDtypeStruct((N, D), F32),
        scratch_shapes=[pltpu.VMEM((D, dh), BF16), pltpu.VMEM((D, dh), BF16), pltpu.VMEM((dh, D), BF16)],
        compiler_params=_params(("arbitrary",)),
        name="shared_expert",
    )(f, wg, wu, wd)


def _packed_rows(ref, first_row, n_rows):
    start = first_row * ROW_TILES
    if not isinstance(start, int):
        start = pl.multiple_of(start, ROW_TILES)
    return ref.at[pl.ds(start, n_rows * ROW_TILES)]


def _issue_rows(src_hbm, idx_smem, slot, idx_off, dst, dst_row, sem, n_rows, unroll):
    def issue(r, carry):
        t = idx_smem[slot, 0, idx_off + r]
        pltpu.make_async_copy(_packed_rows(src_hbm, t, 1), _packed_rows(dst, dst_row + r, 1), sem).start()
        return carry

    lax.fori_loop(0, n_rows, issue, 0, unroll=unroll)


def _wait_rows(src_hbm, dst, dst_row, sem, n_rows):
    pltpu.make_async_copy(_packed_rows(src_hbm, 0, n_rows), _packed_rows(dst, dst_row, n_rows), sem).wait()


def _routed_kernel(be_ref, ne_ref, nu_ref, tok_hbm, x_hbm, wg_hbm, wu_hbm, wd_hbm, y_ref,
                   idx_smem, xbuf0, xbuf1, wgs, wus, wds, wgb, wub, wdb, sem_i, sem_g, sem_w, *, layer):
    j = pl.program_id(0)
    n_used = nu_ref[0]
    last_blk = pl.num_programs(0) - 1
    xbufs = (xbuf0, xbuf1)

    def idx_copy(blk, s):
        return pltpu.make_async_copy(tok_hbm.at[jnp.minimum(blk, last_blk)], idx_smem.at[s], sem_i.at[s])

    def weight_copies(e):
        return [pltpu.make_async_copy(src.at[layer, e], dst, sem_w)
                for src, dst in ((wg_hbm, wgs), (wu_hbm, wus), (wd_hbm, wds))]

    def rows_wait(s):
        _wait_rows(x_hbm, xbufs[s], 0, sem_g.at[s], MOE_BLOCK)

    @pl.when(jnp.logical_and(j == 0, n_used > 0))
    def _():
        first = idx_copy(0, 0)
        first.start()
        first.wait()
        _issue_rows(x_hbm, idx_smem, 0, 0, xbuf0, 0, sem_g.at[0], MOE_BLOCK, 8)
        idx_copy(1, 1).start()
        for cp in weight_copies(be_ref[0]):
            cp.start()

    @pl.when(j >= n_used)
    def _():
        y_ref[...] = jnp.zeros_like(y_ref)

    @pl.when(j < n_used)
    def _():
        e = be_ref[j]

        @pl.when(jnp.logical_or(j == 0, e != be_ref[jnp.maximum(j - 1, 0)]))
        def _():
            for cp in weight_copies(e):
                cp.wait()
            wgb[...] = wgs[...].astype(BF16)
            wub[...] = wus[...].astype(BF16)
            wdb[...] = wds[...].astype(BF16)

            @pl.when(ne_ref[j] != e)
            def _():
                for cp in weight_copies(ne_ref[j]):
                    cp.start()

        for cur in (0, 1):
            nxt = 1 - cur

            @pl.when(j % 2 == cur)
            def _():
                idx_copy(j + 1, nxt).wait()
                rows_wait(cur)
                for r in range(MOE_BLOCK):
                    t = idx_smem[nxt, 0, r]
                    pltpu.make_async_copy(_packed_rows(x_hbm, t, 1), _packed_rows(xbufs[nxt], r, 1),
                                          sem_g.at[nxt]).start()
                pieces = [_load_packed(xbufs[cur], 0, MOE_BLOCK, s) for s in range(ROW_TILES)]
                x = jnp.concatenate([p[0].astype(BF16) for p in pieces] + [p[1].astype(BF16) for p in pieces],
                                    axis=1)
                _store_packed(y_ref, _swiglu(x, wgb[...], wub[...], wdb[...]))
                idx_copy(j + 2, cur).start()

                @pl.when(j == n_used - 1)
                def _():
                    rows_wait(nxt)
                    idx_copy(j + 2, cur).wait()


def routed_experts(x_packed, tok_buf, block_expert, next_expert, n_used, wg, wu, wd, layer):
    n_blocks = block_expert.shape[0]
    dh = wg.shape[3]
    any_spec = pl.BlockSpec(memory_space=pl.ANY)
    grid_spec = pltpu.PrefetchScalarGridSpec(
        num_scalar_prefetch=3,
        grid=(n_blocks,),
        in_specs=[any_spec] * 5,
        out_specs=pl.BlockSpec((MOE_BLOCK * ROW_TILES, LANES), lambda j, be, ne, nu: (j, 0)),
        scratch_shapes=[
            pltpu.SMEM((2, 1, MOE_BLOCK), jnp.int32),
            pltpu.VMEM((MOE_BLOCK * ROW_TILES, LANES), jnp.uint32),
            pltpu.VMEM((MOE_BLOCK * ROW_TILES, LANES), jnp.uint32),
            pltpu.VMEM((D, dh), F32), pltpu.VMEM((D, dh), F32), pltpu.VMEM((dh, D), F32),
            pltpu.VMEM((D, dh), BF16), pltpu.VMEM((D, dh), BF16), pltpu.VMEM((dh, D), BF16),
            pltpu.SemaphoreType.DMA((2,)), pltpu.SemaphoreType.DMA((2,)), pltpu.SemaphoreType.DMA,
        ],
    )
    return pl.pallas_call(
        functools.partial(_routed_kernel, layer=layer),
        grid_spec=grid_spec,
        out_shape=jax.ShapeDtypeStruct((n_blocks * MOE_BLOCK * ROW_TILES, LANES), jnp.uint32),
        compiler_params=_params(("arbitrary",)),
        name="routed_experts",
    )(block_expert, next_expert, n_used, tok_buf, x_packed, wg, wu, wd)


COMBINE_T = 128


def _combine_kernel(dest_hbm, ys_hbm, wt_ref, sh_ref, h_ref, gate_ref, o_ref, idx_smem, gbuf, sem_i, sem_g):
    i = pl.program_id(0)
    n_tiles = pl.num_programs(0)
    slot = i % 2
    nxt = 1 - slot

    def idx_copy(tile, s):
        return pltpu.make_async_copy(dest_hbm.at[tile], idx_smem.at[s], sem_i.at[s])

    def issue_tile(s):
        _issue_rows(ys_hbm, idx_smem, s, 0, gbuf, s * TOP_K * COMBINE_T, sem_g.at[s], TOP_K * COMBINE_T, 8)

    @pl.when(i == 0)
    def _():
        first = idx_copy(0, 0)
        first.start()
        first.wait()
        issue_tile(0)
        idx_copy(1, 1).start()

    @pl.when(i + 1 < n_tiles)
    def _():
        idx_copy(i + 1, nxt).wait()
        issue_tile(nxt)

    @pl.when(i + 2 < n_tiles)
    def _():
        idx_copy(i + 2, slot).start()

    _wait_rows(ys_hbm, gbuf, slot * TOP_K * COMBINE_T, sem_g.at[slot], TOP_K * COMBINE_T)
    wts = [wt_ref[:, k:k + 1] for k in range(TOP_K)]
    for s in range(ROW_TILES):
        c_lo = slice(s * LANES, (s + 1) * LANES)
        c_hi = slice(HALF + s * LANES, HALF + (s + 1) * LANES)
        acc_lo = sh_ref[:, c_lo]
        acc_hi = sh_ref[:, c_hi]
        for k in range(TOP_K):
            lo, hi = _load_packed(gbuf, (slot * TOP_K + k) * COMBINE_T, COMBINE_T, s)
            acc_lo = acc_lo + wts[k] * lo
            acc_hi = acc_hi + wts[k] * hi
        o_ref[:, c_lo] = h_ref[:, c_lo] + gate_ref[:, c_lo] * acc_lo
        o_ref[:, c_hi] = h_ref[:, c_hi] + gate_ref[:, c_hi] * acc_hi


def combine(dest, ys, wts, shared, h, mod3, gate_idx):
    per_tt = TT // COMBINE_T
    return pl.pallas_call(
        _combine_kernel,
        grid=(N // COMBINE_T,),
        in_specs=[
            pl.BlockSpec(memory_space=pl.ANY),
            pl.BlockSpec(memory_space=pl.ANY),
            pl.BlockSpec((COMBINE_T, TOP_K), lambda i: (i, 0)),
            pl.BlockSpec((COMBINE_T, D), lambda i: (i, 0)),
            pl.BlockSpec((COMBINE_T, D), lambda i: (i, 0)),
            pl.BlockSpec((None, 1, D), lambda i: (_mod_row(i // per_tt) * N_MOD + gate_idx, 0, 0)),
        ],
        out_specs=pl.BlockSpec((COMBINE_T, D), lambda i: (i, 0)),
        out_shape=jax.ShapeDtypeStruct((N, D), F32),
        scratch_shapes=[
            pltpu.SMEM((2, 1, TOP_K * COMBINE_T), jnp.int32),
            pltpu.VMEM((2 * TOP_K * COMBINE_T * ROW_TILES, LANES), jnp.uint32),
            pltpu.SemaphoreType.DMA((2,)), pltpu.SemaphoreType.DMA((2,)),
        ],
        compiler_params=_params(("arbitrary",)),
        name="combine",
    )(dest, ys, wts, shared, h, mod3)


def moe_layer(h, f, f_packed, mod3, gate_idx, layer, w_router, b_router, wg, wu, wd, wsg, wsu, wsd):
    idx, wts, rank, counts = router(f, w_router, b_router)
    counts = counts[:, 0]
    padded = (counts + MOE_BLOCK - 1) // MOE_BLOCK * MOE_BLOCK
    pad_end = jnp.cumsum(padded)
    pad_start = pad_end - padded
    onehot = idx[:, :, None] == jnp.arange(N_EXPERTS, dtype=jnp.int32)[None, None, :]
    dest = jnp.sum(jnp.where(onehot, pad_start[None, None, :], 0), axis=-1) + rank
    n_blocks = -(-(N * TOP_K + N_EXPERTS * (MOE_BLOCK - 1)) // MOE_BLOCK)
    cap = n_blocks * MOE_BLOCK
    tok = jnp.broadcast_to(jnp.arange(N, dtype=jnp.int32)[None, :], dest.shape)
    filler = jnp.arange(cap, dtype=jnp.int32) % N
    tok_buf = filler.at[dest.reshape(-1)].set(tok.reshape(-1))
    block_start = jnp.arange(n_blocks, dtype=jnp.int32) * MOE_BLOCK
    block_expert = jnp.sum(block_start[:, None] >= pad_end[None, :], axis=1)
    block_expert = jnp.minimum(block_expert, N_EXPERTS - 1).astype(jnp.int32)
    n_used = (pad_end[-1:] // MOE_BLOCK).astype(jnp.int32)
    e_ids = jnp.arange(N_EXPERTS, dtype=jnp.int32)
    later = jnp.logical_and(e_ids[None, :] > e_ids[:, None], counts[None, :] > 0)
    next_by_expert = jnp.min(jnp.where(later, e_ids[None, :], N_EXPERTS), axis=1)
    next_by_expert = jnp.where(next_by_expert == N_EXPERTS, e_ids, next_by_expert)
    next_expert = jnp.sum(jnp.where(block_expert[:, None] == e_ids[None, :], next_by_expert[None, :], 0),
                          axis=1).astype(jnp.int32)
    ys = routed_experts(f_packed, tok_buf.reshape(n_blocks, 1, MOE_BLOCK), block_expert, next_expert, n_used,
                        wg, wu, wd, layer)
    shared = shared_expert(f, wsg, wsu, wsd, layer)
    dest_tiles = dest.reshape(TOP_K, N // COMBINE_T, COMBINE_T).transpose(1, 0, 2)
    dest_tiles = dest_tiles.reshape(N // COMBINE_T, 1, TOP_K * COMBINE_T)
    return combine(dest_tiles, ys, wts.T, shared, h, mod3, gate_idx)


def _rope_tables(rot_dim):
    rows = T // GRID_W
    row = np.repeat(np.arange(rows, dtype=np.float32), GRID_W)
    col = np.tile(np.arange(GRID_W, dtype=np.float32), rows)
    half = rot_dim // 2
    inv_freq = jnp.asarray(ROPE_THETA, F32) ** (-jnp.arange(0, half, 2, dtype=F32) / half)
    ang_r = jnp.asarray(row)[:, None] * inv_freq[None, :]
    ang_c = jnp.asarray(col)[:, None] * inv_freq[None, :]
    ang = jnp.concatenate([ang_r, ang_r, ang_c, ang_c], axis=-1)
    cos, sin = jnp.cos(ang), jnp.sin(ang)
    quarter = (np.arange(rot_dim) // (rot_dim // 4)) % 2
    s1 = jnp.where(quarter[None, :] == 0, -sin, 0.0)
    s2 = jnp.where(quarter[None, :] == 1, sin, 0.0)

    def full(tbl, fill):
        tbl = jnp.pad(tbl, ((0, 0), (0, LANES - rot_dim)), constant_values=fill)
        return jnp.concatenate([jnp.full((CTX, LANES), fill, F32), tbl], axis=0)

    return full(cos, 1.0), full(s1, 0.0), full(s2, 0.0)


def _pad_heads(w, n_heads, width, new_width):
    k = w.shape[0]
    w = w.reshape(k, n_heads, width)
    return jnp.pad(w, ((0, 0), (0, 0), (0, new_width - width))).reshape(k, n_heads * new_width)


def _pad_gain(g, new_width):
    return jnp.pad(g, (0, new_width - g.shape[0])).reshape(1, new_width)


def mla_mixer(a, tables, w_down, g_q, g_kv, w_uq, w_ukv, g_qn, g_kn, w_o_args):
    c, s1, s2 = tables
    w_dq = w_down[:, :MLA_Q_RANK]
    w_dkv = w_down[:, MLA_Q_RANK:MLA_Q_RANK + MLA_KV_RANK]
    w_dr = jnp.pad(w_down[:, MLA_Q_RANK + MLA_KV_RANK:], ((0, 0), (0, LANES - MLA_ROPE)))
    cq = matmul(a, w_dq, tn=MLA_Q_RANK, extra=(g_q.reshape(1, -1),), extra_specs=(_const_spec(MLA_Q_RANK),),
                out_shapes=[jax.ShapeDtypeStruct((N, MLA_Q_RANK), BF16)], out_specs=[_row_spec(MLA_Q_RANK)],
                epilogue=_epi_rmsnorm, name="mla_down_q")[0]
    ckv = matmul(a, w_dkv, tn=MLA_KV_RANK, extra=(g_kv.reshape(1, -1),),
                 extra_specs=(_const_spec(MLA_KV_RANK),),
                 out_shapes=[jax.ShapeDtypeStruct((N, MLA_KV_RANK), BF16)], out_specs=[_row_spec(MLA_KV_RANK)],
                 epilogue=_epi_rmsnorm, name="mla_down_kv")[0]
    k_rope = matmul(a, w_dr, tn=LANES, out_shapes=[jax.ShapeDtypeStruct((N, LANES), F32)],
                    out_specs=[_row_spec(LANES)], epilogue=_epi_plain, name="mla_down_rope")[0]

    tn = 1024
    heads_t = tn // MLA_HW
    scale = MLA_QK ** -0.5 * LOG2E
    w_uq_p = _pad_heads(w_uq, MLA_HEADS, MLA_QK, MLA_HW)
    q = matmul(cq, w_uq_p, tn=tn, extra=(_pad_gain(g_qn * scale, MLA_HW), c, s1, s2),
               extra_specs=(_const_spec(MLA_HW), _table_spec(), _table_spec(), _table_spec()),
               out_shapes=[jax.ShapeDtypeStruct((N, MLA_HEADS * MLA_HW), BF16)], out_specs=[_row_spec(tn)],
               epilogue=functools.partial(_epi_mla_q, n_heads=heads_t), name="mla_up_q")[0]
    k, v = matmul(ckv, w_ukv, tn=tn, extra=(k_rope, _pad_gain(g_kn, MLA_HW), c, s1, s2),
                  extra_specs=(pl.BlockSpec((TT, LANES), lambda n, i: (i, 0)), _const_spec(MLA_HW),
                               _table_spec(), _table_spec(), _table_spec()),
                  out_shapes=[jax.ShapeDtypeStruct((N, MLA_HEADS * MLA_HW), BF16),
                              jax.ShapeDtypeStruct((N, MLA_HEADS * MLA_V), BF16)],
                  out_specs=[_row_spec(tn), _row_spec(tn // 2)],
                  epilogue=functools.partial(_epi_mla_kv, n_heads=heads_t), name="mla_up_kv")
    heads_step = 4
    o = attention(q, k, v, n_steps=MLA_HEADS // heads_step, n_heads=heads_step, kv_shared=False,
                  dqk=MLA_HW, dv=MLA_V, with_ctx=True)
    return _out_proj(o, *w_o_args)


def gqa_mixer(a, tables, w_qkv, g_qn, g_kn, w_o_args):
    c, s1, s2 = tables
    tn = 512
    heads_t = tn // GQA_HD
    scale = GQA_HD ** -0.5 * LOG2E
    rope_specs = (_const_spec(GQA_HD), _table_spec(), _table_spec(), _table_spec())
    nq = GQA_HEADS * GQA_HD
    nk = GQA_KV_HEADS * GQA_HD
    q = matmul(a, w_qkv, tn=tn, n_cols=nq, extra=((g_qn * scale).reshape(1, -1), c, s1, s2),
               extra_specs=rope_specs, out_shapes=[jax.ShapeDtypeStruct((N, nq), BF16)],
               out_specs=[_row_spec(tn)], epilogue=functools.partial(_epi_head_rope, n_heads=heads_t),
               name="gqa_q")[0]
    k = matmul(a, w_qkv, tn=tn, col_off=nq // tn, n_cols=nk, extra=(g_kn.reshape(1, -1), c, s1, s2),
               extra_specs=rope_specs, out_shapes=[jax.ShapeDtypeStruct((N, nk), BF16)],
               out_specs=[_row_spec(tn)], epilogue=functools.partial(_epi_head_rope, n_heads=heads_t),
               name="gqa_k")[0]
    v = matmul(a, w_qkv, tn=tn, col_off=(nq + nk) // tn, n_cols=nk,
               out_shapes=[jax.ShapeDtypeStruct((N, nk), BF16)], out_specs=[_row_spec(tn)],
               epilogue=_epi_plain, name="gqa_v")[0]
    o = attention(q, k, v, n_steps=GQA_KV_HEADS, n_heads=GQA_REP, kv_shared=True, dqk=GQA_HD, dv=GQA_HD,
                  with_ctx=True)
    return _out_proj(o, *w_o_args)


def _out_proj(o, w_o, h, mod3, gate_idx):
    tn = 1024
    return matmul(o, w_o, tn=tn, extra=(h, mod3),
                  extra_specs=(_row_spec(tn),
                               pl.BlockSpec((None, 1, tn), lambda n, i: (_mod_row(i) * N_MOD + gate_idx, 0, n))),
                  out_shapes=[jax.ShapeDtypeStruct((N, D), F32)], out_specs=[_row_spec(tn)],
                  epilogue=_epi_resgate, name="out_proj")[0]


def kernel(x, c, ctx, c_ctx, ada_w, ada_b, norm1_g, norm2_g, mla_w_down, mla_g_q, mla_g_kv, mla_w_uq,
           mla_w_ukv, mla_g_qn, mla_g_kn, mla_w_o, gqa_w_qkv, gqa_g_qn, gqa_g_kn, gqa_w_o, moe_w_router,
           moe_b_router, moe_w_gate, moe_w_up, moe_w_down, moe_ws_gate, moe_ws_up, moe_ws_down):
    depth = ada_w.shape[0]
    h = jnp.concatenate([ctx, x], axis=1).reshape(N, D)
    c_rows = jnp.concatenate([c_ctx[None, :], c, jnp.zeros((16 - 1 - B, D), F32)], axis=0)
    mod = ada_table(c_rows, ada_w, ada_b)
    tables_a = _rope_tables(MLA_ROPE)
    tables_b = _rope_tables(GQA_HD)
    for i in range(depth):
        mod3 = mod[i].reshape(16 * N_MOD, 1, D)
        a = norm_mod(h, norm1_g[i], mod3, 0, 1, BF16)
        j = i // 2
        if i % 2 == 0:
            h = mla_mixer(a, tables_a, mla_w_down[j], mla_g_q[j], mla_g_kv[j], mla_w_uq[j], mla_w_ukv[j],
                          mla_g_qn[j], mla_g_kn[j], (mla_w_o[j], h, mod3, 2))
        else:
            h = gqa_mixer(a, tables_b, gqa_w_qkv[j], gqa_g_qn[j], gqa_g_kn[j], (gqa_w_o[j], h, mod3, 2))
        f, f_packed = norm_mod(h, norm2_g[i], mod3, 3, 4, F32, with_packed=True)
        h = moe_layer(h, f, f_packed, mod3, 5, i, moe_w_router[i], moe_b_router[i], moe_w_gate, moe_w_up,
                      moe_w_down, moe_ws_gate, moe_ws_up, moe_ws_down)
    return h.reshape(B, S, D)[:, CTX:, :]
```

```python
import functools

import jax
import jax.numpy as jnp
import numpy as np
from jax import lax
from jax.experimental import pallas as pl
from jax.experimental.pallas import tpu as pltpu

F32 = jnp.float32
BF16 = jnp.bfloat16
HIGHEST = lax.Precision.HIGHEST

D = 2048
B = 8
T = 2048
CTX = 256
S = CTX + T
N = B * S
TT = 256
TILES_B = S // TT
GRID_W = 64
ROPE_THETA = 10000.0
EPS = 1e-6
N_MOD = 6

MLA_HEADS = 16
MLA_Q_RANK = 768
MLA_KV_RANK = 512
MLA_NOPE = 128
MLA_ROPE = 64
MLA_QK = MLA_NOPE + MLA_ROPE
MLA_V = 128
MLA_HW = 256

GQA_HEADS = 16
GQA_KV_HEADS = 4
GQA_REP = GQA_HEADS // GQA_KV_HEADS
GQA_HD = 128

N_EXPERTS = 64
N_GROUPS = 8
GROUP_SIZE = N_EXPERTS // N_GROUPS
TOPK_GROUPS = 4
TOP_K = 8
D_EXPERT = 512
ROUTED_SCALE = 2.5
MOE_BLOCK = 256

LANES = 128
LOG2E = 1.4426950408889634
VMEM_LIMIT = 56 * 1024 * 1024


def _params(sem, vmem=VMEM_LIMIT):
    return pltpu.CompilerParams(dimension_semantics=sem, vmem_limit_bytes=vmem)


def _mod_row(i):
    return jnp.where(i % TILES_B == 0, 0, i // TILES_B + 1)


def _ada_kernel(c_ref, w_ref, b_ref, o_ref):
    a = c_ref[...]
    a = a * jax.nn.sigmoid(a)
    o_ref[...] = jnp.dot(a, w_ref[...], preferred_element_type=F32, precision=HIGHEST) + b_ref[...]


def ada_table(c_rows, ada_w, ada_b):
    depth = ada_w.shape[0]
    tn = 1024
    return pl.pallas_call(
        _ada_kernel,
        grid=(depth, N_MOD * D // tn),
        in_specs=[
            pl.BlockSpec((16, D), lambda l, n: (0, 0)),
            pl.BlockSpec((None, D, tn), lambda l, n: (l, 0, n)),
            pl.BlockSpec((None, 1, tn), lambda l, n: (l, 0, n)),
        ],
        out_specs=pl.BlockSpec((None, 16, tn), lambda l, n: (l, 0, n)),
        out_shape=jax.ShapeDtypeStruct((depth, 16, N_MOD * D), F32),
        compiler_params=_params(("parallel", "parallel")),
        name="ada_table",
    )(c_rows, ada_w, ada_b.reshape(depth, 1, N_MOD * D))


HALF = D // 2
ROW_TILES = HALF // LANES


def _store_packed(ref, y):
    rows = y.shape[0]
    for s in range(ROW_TILES):
        lo = y[:, s * LANES:(s + 1) * LANES]
        hi = y[:, HALF + s * LANES:HALF + (s + 1) * LANES]
        ref[pl.ds(s, rows, stride=ROW_TILES), :] = pltpu.pack_elementwise([lo, hi], packed_dtype=BF16)


def _load_packed(ref, base, rows, s, row_stride=1):
    u = ref[pl.ds(base * ROW_TILES + s, rows, stride=row_stride * ROW_TILES), :]
    lo = pltpu.unpack_elementwise(u, index=0, packed_dtype=BF16, unpacked_dtype=F32)
    hi = pltpu.unpack_elementwise(u, index=1, packed_dtype=BF16, unpacked_dtype=F32)
    return lo, hi


def _norm_mod_kernel(h_ref, g_ref, sh_ref, sc_ref, o_ref, *packed_ref):
    x = h_ref[...]
    ms = jnp.mean(x * x, axis=-1, keepdims=True)
    y = x * lax.rsqrt(ms + EPS) * g_ref[...]
    y = y * (1.0 + sc_ref[...]) + sh_ref[...]
    o_ref[...] = y.astype(o_ref.dtype)
    if packed_ref:
        _store_packed(packed_ref[0], y)


def norm_mod(h, gain, mod3, shift_idx, scale_idx, out_dtype, with_packed=False):
    row = pl.BlockSpec((TT, D), lambda i: (i, 0))
    out_specs, out_shape = [row], [jax.ShapeDtypeStruct((N, D), out_dtype)]
    if with_packed:
        out_specs.append(pl.BlockSpec((TT * ROW_TILES, LANES), lambda i: (i, 0)))
        out_shape.append(jax.ShapeDtypeStruct((N * ROW_TILES, LANES), jnp.uint32))
    outs = pl.pallas_call(
        _norm_mod_kernel,
        grid=(N // TT,),
        in_specs=[
            row,
            pl.BlockSpec((1, D), lambda i: (0, 0)),
            pl.BlockSpec((None, 1, D), lambda i: (_mod_row(i) * N_MOD + shift_idx, 0, 0)),
            pl.BlockSpec((None, 1, D), lambda i: (_mod_row(i) * N_MOD + scale_idx, 0, 0)),
        ],
        out_specs=out_specs,
        out_shape=out_shape,
        compiler_params=_params(("parallel",)),
        name="norm_mod",
    )(h, gain.reshape(1, D), mod3, mod3)
    return outs if with_packed else outs[0]


MXU_COLS = 256


def _mm_kernel(*refs, n_extra, n_out, epilogue, chunk):
    a_ref, w_ref = refs[0], refs[1]
    extra = refs[2:2 + n_extra]
    outs = refs[2 + n_extra:2 + n_extra + n_out]
    wb_ref = refs[2 + n_extra + n_out]

    @pl.when(pl.program_id(1) == 0)
    def _():
        wb_ref[...] = w_ref[...].astype(BF16)

    a = a_ref[...].astype(BF16)
    if chunk is None:
        epilogue(jnp.dot(a, wb_ref[...], preferred_element_type=F32), extra, outs, 0)
    else:
        for col in range(0, wb_ref.shape[1], chunk):
            acc = jnp.dot(a, wb_ref[:, col:col + chunk], preferred_element_type=F32)
            epilogue(acc, extra, outs, col)


def matmul(a, w, *, tn, col_off=0, n_cols=None, extra=(), extra_specs=(), out_shapes, out_specs, epilogue,
           name, chunk=MXU_COLS):
    k = a.shape[1]
    n_cols = w.shape[1] if n_cols is None else n_cols
    kern = functools.partial(_mm_kernel, n_extra=len(extra), n_out=len(out_shapes), epilogue=epilogue,
                             chunk=chunk)
    return pl.pallas_call(
        kern,
        grid=(n_cols // tn, N // TT),
        in_specs=[
            pl.BlockSpec((TT, k), lambda n, i: (i, 0)),
            pl.BlockSpec((k, tn), lambda n, i: (0, n + col_off)),
            *extra_specs,
        ],
        out_specs=out_specs,
        out_shape=out_shapes,
        scratch_shapes=[pltpu.VMEM((k, tn), BF16)],
        compiler_params=_params(("parallel", "arbitrary")),
        name=name,
    )(a, w, *extra)


def _epi_plain(acc, extra, outs, col):
    outs[0][:, col:col + acc.shape[1]] = acc.astype(outs[0].dtype)


def _epi_rmsnorm(acc, extra, outs, col):
    (g_ref,) = extra
    ms = jnp.mean(acc * acc, axis=-1, keepdims=True)
    outs[0][...] = (acc * lax.rsqrt(ms + EPS) * g_ref[...]).astype(outs[0].dtype)


def _epi_resgate(acc, extra, outs, col):
    res_ref, gate_ref = extra
    cols = slice(col, col + acc.shape[1])
    outs[0][:, cols] = res_ref[:, cols] + gate_ref[:, cols] * acc


def _rope(x, c, s1, s2, quarter):
    return x * c + pltpu.roll(x, LANES - quarter, 1) * s1 + pltpu.roll(x, quarter, 1) * s2


def _epi_head_rope(acc, extra, outs, col):
    g_ref, c_ref, s1_ref, s2_ref = extra
    c, s1, s2 = c_ref[...], s1_ref[...], s2_ref[...]
    for j in range(acc.shape[1] // LANES):
        x = acc[:, j * LANES:(j + 1) * LANES]
        ms = jnp.mean(x * x, axis=-1, keepdims=True)
        xn = x * lax.rsqrt(ms + EPS) * g_ref[...]
        dst = slice(col + j * LANES, col + (j + 1) * LANES)
        outs[0][:, dst] = _rope(xn, c, s1, s2, GQA_HD // 4).astype(outs[0].dtype)


def _mla_head(nope, rope, g_ref, c, s1, s2):
    ss = jnp.sum(nope * nope, axis=-1, keepdims=True) + jnp.sum(rope * rope, axis=-1, keepdims=True)
    r = lax.rsqrt(ss * (1.0 / MLA_QK) + EPS)
    nope_n = nope * r * g_ref[:, :LANES]
    rope_n = _rope(rope * r * g_ref[:, LANES:], c, s1, s2, MLA_ROPE // 4)
    return nope_n, rope_n


def _epi_mla_q(acc, extra, outs, col):
    g_ref, c_ref, s1_ref, s2_ref = extra
    c, s1, s2 = c_ref[...], s1_ref[...], s2_ref[...]
    for j in range(acc.shape[1] // MLA_HW):
        nope = acc[:, j * MLA_HW:j * MLA_HW + LANES]
        rope = acc[:, j * MLA_HW + LANES:(j + 1) * MLA_HW]
        nope_n, rope_n = _mla_head(nope, rope, g_ref, c, s1, s2)
        dst = col + j * MLA_HW
        outs[0][:, dst:dst + LANES] = nope_n.astype(outs[0].dtype)
        outs[0][:, dst + LANES:dst + MLA_HW] = rope_n.astype(outs[0].dtype)


def _epi_mla_kv(acc, extra, outs, col):
    kr_ref, g_ref, c_ref, s1_ref, s2_ref = extra
    k_out, v_out = outs
    c, s1, s2 = c_ref[...], s1_ref[...], s2_ref[...]
    rope = kr_ref[...]
    for j in range(acc.shape[1] // MLA_HW):
        nope = acc[:, j * MLA_HW:j * MLA_HW + LANES]
        v = acc[:, j * MLA_HW + LANES:(j + 1) * MLA_HW]
        nope_n, rope_n = _mla_head(nope, rope, g_ref, c, s1, s2)
        dst = col + j * MLA_HW
        k_out[:, dst:dst + LANES] = nope_n.astype(k_out.dtype)
        k_out[:, dst + LANES:dst + MLA_HW] = rope_n.astype(k_out.dtype)
        v_out[:, dst // 2:dst // 2 + LANES] = v.astype(v_out.dtype)


def _row_spec(width):
    return pl.BlockSpec((TT, width), lambda n, i: (i, n))


def _const_spec(width):
    return pl.BlockSpec((1, width), lambda n, i: (0, 0))


def _table_spec():
    return pl.BlockSpec((TT, LANES), lambda n, i: (i % TILES_B, 0))


KEY_CHUNK = 768


def _attn_kernel(q_ref, k_ref, v_ref, o_ref, *, n_heads, kv_shared, dqk, dv, with_ctx):
    def compute(nk):
        chunk = min(KEY_CHUNK, nk)
        n_chunks = nk // chunk

        def scores(h, c):
            kh = 0 if kv_shared else h
            q = q_ref[:, h * dqk:(h + 1) * dqk]
            k = k_ref[c * chunk:(c + 1) * chunk, kh * dqk:(kh + 1) * dqk]
            return lax.dot_general(q, k, (((1,), (1,)), ((), ())), preferred_element_type=F32)

        def row_max(s_chunks):
            m = jnp.max(s_chunks[0], axis=-1, keepdims=True)
            for s in s_chunks[1:]:
                m = jnp.maximum(m, jnp.max(s, axis=-1, keepdims=True))
            return m

        s_cur = [scores(0, c) for c in range(n_chunks)]
        for h in range(n_heads):
            vh = 0 if kv_shared else h
            m = row_max(s_cur)
            s_next = []
            l = None
            o = None
            for c in range(n_chunks):
                p = jnp.exp2(s_cur[c] - m)
                l_c = jnp.sum(p, axis=-1, keepdims=True)
                v = v_ref[c * chunk:(c + 1) * chunk, vh * dv:(vh + 1) * dv]
                o_c = jnp.dot(p.astype(BF16), v, preferred_element_type=F32)
                l = l_c if l is None else l + l_c
                o = o_c if o is None else o + o_c
                if h + 1 < n_heads:
                    s_next.append(scores(h + 1, c))
            o_ref[:, h * dv:(h + 1) * dv] = (o / l).astype(o_ref.dtype)
            s_cur = s_next

    if with_ctx:
        @pl.when(pl.program_id(2) == 0)
        def _():
            compute(CTX)

        @pl.when(pl.program_id(2) > 0)
        def _():
            compute(S)
    else:
        compute(S)


def attention(q, k, v, *, n_steps, n_heads, kv_shared, dqk, dv, with_ctx):
    first = 0 if with_ctx else 1
    kv_heads = 1 if kv_shared else n_heads
    kern = functools.partial(_attn_kernel, n_heads=n_heads, kv_shared=kv_shared, dqk=dqk, dv=dv,
                             with_ctx=with_ctx)
    return pl.pallas_call(
        kern,
        grid=(B, n_steps, TILES_B - first),
        in_specs=[
            pl.BlockSpec((TT, n_heads * dqk), lambda b, g, i: (b * TILES_B + first + i, g)),
            pl.BlockSpec((S, kv_heads * dqk), lambda b, g, i: (b, g)),
            pl.BlockSpec((S, kv_heads * dv), lambda b, g, i: (b, g)),
        ],
        out_specs=pl.BlockSpec((TT, n_heads * dv), lambda b, g, i: (b * TILES_B + first + i, g)),
        out_shape=jax.ShapeDtypeStruct((N, n_steps * n_heads * dv), BF16),
        compiler_params=_params(("parallel", "parallel", "arbitrary")),
        name="attention",
    )(q, k, v)


def _first_index(hit, iota, size):
    return jnp.min(jnp.where(hit, iota, float(size)), axis=0, keepdims=True)


def _router_kernel(f_ref, wr_ref, br_ref, idx_ref, wt_ref, rank_ref, cnt_ref, carry_ref):
    @pl.when(pl.program_id(0) == 0)
    def _():
        carry_ref[...] = jnp.zeros_like(carry_ref)

    neg = -jnp.inf
    logits = lax.dot_general(wr_ref[...], f_ref[...], (((1,), (1,)), ((), ())),
                             preferred_element_type=F32, precision=HIGHEST)
    scores = jax.nn.sigmoid(logits)
    biased = scores + br_ref[...]

    iota_m = lax.broadcasted_iota(jnp.int32, (GROUP_SIZE, TT), 0).astype(F32)
    groups = [biased[g * GROUP_SIZE:(g + 1) * GROUP_SIZE, :] for g in range(N_GROUPS)]
    gs_rows = []
    for blk in groups:
        m1 = jnp.max(blk, axis=0, keepdims=True)
        i1 = _first_index(blk == m1, iota_m, GROUP_SIZE)
        m2 = jnp.max(jnp.where(iota_m == i1, neg, blk), axis=0, keepdims=True)
        gs_rows.append(m1 + m2)
    gs = jnp.concatenate(gs_rows, axis=0)

    iota_g = lax.broadcasted_iota(jnp.int32, gs.shape, 0).astype(F32)
    sel = jnp.zeros(gs.shape, F32)
    cur = gs
    for _ in range(TOPK_GROUPS):
        m = jnp.max(cur, axis=0, keepdims=True)
        hit = iota_g == _first_index(cur == m, iota_g, N_GROUPS)
        sel = jnp.where(hit, 1.0, sel)
        cur = jnp.where(hit, neg, cur)

    cur = jnp.concatenate(
        [jnp.where(sel[g:g + 1, :] > 0.5, groups[g], neg) for g in range(N_GROUPS)], axis=0)
    iota_e = lax.broadcasted_iota(jnp.int32, cur.shape, 0).astype(F32)
    assigned = jnp.zeros(cur.shape, F32)
    w_rows, hits = [], []
    for k in range(TOP_K):
        m = jnp.max(cur, axis=0, keepdims=True)
        first = _first_index(cur == m, iota_e, N_EXPERTS)
        hit = iota_e == first
        idx_ref[k:k + 1, :] = first.astype(jnp.int32)
        w_rows.append(jnp.sum(jnp.where(hit, scores, 0.0), axis=0, keepdims=True))
        hits.append(hit)
        assigned = jnp.where(hit, 1.0, assigned)
        cur = jnp.where(hit, neg, cur)

    w_sum = w_rows[0]
    for k in range(1, TOP_K):
        w_sum = w_sum + w_rows[k]
    for k in range(TOP_K):
        wt_ref[k:k + 1, :] = w_rows[k] / w_sum * ROUTED_SCALE

    r_i = lax.broadcasted_iota(jnp.int32, (TT, TT), 0)
    c_i = lax.broadcasted_iota(jnp.int32, (TT, TT), 1)
    upper = jnp.where(r_i <= c_i, 1.0, 0.0).astype(BF16)
    incl = jnp.dot(assigned.astype(BF16), upper, preferred_element_type=F32)
    rank_e = carry_ref[...] + incl - assigned
    for k in range(TOP_K):
        rank_k = jnp.sum(jnp.where(hits[k], rank_e, 0.0), axis=0, keepdims=True)
        rank_ref[k:k + 1, :] = rank_k.astype(jnp.int32)
    carry = carry_ref[...] + jnp.sum(assigned, axis=1, keepdims=True)
    carry_ref[...] = carry
    cnt_ref[...] = carry.astype(jnp.int32)


def router(f, w_router, b_router):
    tok = pl.BlockSpec((TOP_K, TT), lambda i: (0, i))
    return pl.pallas_call(
        _router_kernel,
        grid=(N // TT,),
        in_specs=[
            pl.BlockSpec((TT, D), lambda i: (i, 0)),
            pl.BlockSpec((N_EXPERTS, D), lambda i: (0, 0)),
            pl.BlockSpec((N_EXPERTS, 1), lambda i: (0, 0)),
        ],
        out_specs=[tok, tok, tok, pl.BlockSpec((N_EXPERTS, 1), lambda i: (0, 0))],
        out_shape=[
            jax.ShapeDtypeStruct((TOP_K, N), jnp.int32),
            jax.ShapeDtypeStruct((TOP_K, N), F32),
            jax.ShapeDtypeStruct((TOP_K, N), jnp.int32),
            jax.ShapeDtypeStruct((N_EXPERTS, 1), jnp.int32),
        ],
        scratch_shapes=[pltpu.VMEM((N_EXPERTS, 1), F32)],
        compiler_params=_params(("arbitrary",)),
        name="router",
    )(f, w_router.T, b_router.reshape(N_EXPERTS, 1))


def _swiglu(x, wg, wu, wd):
    g = jnp.dot(x, wg, preferred_element_type=F32)
    u = jnp.dot(x, wu, preferred_element_type=F32)
    mid = (g * jax.nn.sigmoid(g) * u).astype(BF16)
    return jnp.dot(mid, wd, preferred_element_type=F32)


def _packed_rows(ref, first_row, n_rows):
    start = first_row * ROW_TILES
    if not isinstance(start, int):
        start = pl.multiple_of(start, ROW_TILES)
    return ref.at[pl.ds(start, n_rows * ROW_TILES)]


IDX_SLOTS = 4


def _routed_kernel(be_ref, ne_ref, nu_ref, idx_hbm, x_hbm, wg_hbm, wu_hbm, wd_hbm, ys_hbm,
                   idx_smem, xbuf0, xbuf1, ybuf0, ybuf1, wgs, wus, wds, wgb, wub, wdb,
                   sem_i, sem_g, sem_s, sem_w, *, layer):
    j = pl.program_id(0)
    n_used = nu_ref[0]
    n_blocks = pl.num_programs(0)
    xbufs = (xbuf0, xbuf1)
    ybufs = (ybuf0, ybuf1)

    def idx_copy(blk):
        s = blk % IDX_SLOTS
        return pltpu.make_async_copy(idx_hbm.at[jnp.minimum(blk, n_blocks - 1)], idx_smem.at[s], sem_i.at[s])

    def weight_copies(e):
        return [pltpu.make_async_copy(src.at[layer, e], dst, sem_w)
                for src, dst in ((wg_hbm, wgs), (wu_hbm, wus), (wd_hbm, wds))]

    def gather_start(t, p, r):
        pltpu.make_async_copy(_packed_rows(x_hbm, t, 1), _packed_rows(xbufs[p], r, 1), sem_g.at[p]).start()

    def scatter_start(d, p, r):
        pltpu.make_async_copy(_packed_rows(ybufs[p], r, 1), _packed_rows(ys_hbm, d, 1), sem_s.at[p]).start()

    def gather_wait(p):
        pltpu.make_async_copy(_packed_rows(x_hbm, 0, MOE_BLOCK), _packed_rows(xbufs[p], 0, MOE_BLOCK),
                              sem_g.at[p]).wait()

    def scatter_wait(p):
        pltpu.make_async_copy(_packed_rows(ybufs[p], 0, MOE_BLOCK), _packed_rows(ys_hbm, 0, MOE_BLOCK),
                              sem_s.at[p]).wait()

    @pl.when(jnp.logical_and(j == 0, n_used > 0))
    def _():
        first = idx_copy(0)
        first.start()
        first.wait()

        def issue(r, carry):
            gather_start(idx_smem[0, 0, r], 0, r)
            return carry

        lax.fori_loop(0, MOE_BLOCK, issue, 0, unroll=8)
        idx_copy(1).start()
        spare = pltpu.make_async_copy(idx_hbm.at[n_blocks], idx_smem.at[IDX_SLOTS - 1], sem_i.at[IDX_SLOTS - 1])
        spare.start()
        spare.wait()
        ybuf1[...] = jnp.zeros_like(ybuf1)
        for cp in weight_copies(be_ref[0]):
            cp.start()

    @pl.when(j >= n_used)
    def _():
        ybuf0[...] = jnp.zeros_like(ybuf0)
        fill = pltpu.make_async_copy(ybuf0, _packed_rows(ys_hbm, j * MOE_BLOCK, MOE_BLOCK), sem_s.at[0])
        fill.start()
        fill.wait()

    @pl.when(j < n_used)
    def _():
        e = be_ref[j]

        @pl.when(jnp.logical_or(j == 0, e != be_ref[jnp.maximum(j - 1, 0)]))
        def _():
            for cp in weight_copies(e):
                cp.wait()
            wgb[...] = wgs[...].astype(BF16)
            wub[...] = wus[...].astype(BF16)
            wdb[...] = wds[...].astype(BF16)

            @pl.when(ne_ref[j] != e)
            def _():
                for cp in weight_copies(ne_ref[j]):
                    cp.start()

        for cur in (0, 1):
            nxt = 1 - cur

            @pl.when(j % 2 == cur)
            def _():
                @pl.when(j >= 1)
                def _():
                    scatter_wait(cur)

                idx_copy(j + 1).wait()
                gather_wait(cur)
                g_slot = (j + 1) % IDX_SLOTS
                s_slot = (j + IDX_SLOTS - 1) % IDX_SLOTS
                for r in range(MOE_BLOCK):
                    gather_start(idx_smem[g_slot, 0, r], nxt, r)
                    scatter_start(idx_smem[s_slot, 1, r], nxt, r)
                pieces = [_load_packed(xbufs[cur], 0, MOE_BLOCK, s) for s in range(ROW_TILES)]
                x = jnp.concatenate([p[0].astype(BF16) for p in pieces] + [p[1].astype(BF16) for p in pieces],
                                    axis=1)
                _store_packed(ybufs[cur], _swiglu(x, wgb[...], wub[...], wdb[...]))
                idx_copy(j + 2).start()

                @pl.when(j == n_used - 1)
                def _():
                    def issue(r, carry):
                        scatter_start(idx_smem[j % IDX_SLOTS, 1, r], cur, r)
                        return carry

                    lax.fori_loop(0, MOE_BLOCK, issue, 0, unroll=8)
                    scatter_wait(nxt)
                    scatter_wait(cur)
                    gather_wait(nxt)
                    idx_copy(j + 2).wait()


def routed_experts(x_packed, idx_rows, block_expert, next_expert, n_used, wg, wu, wd, layer, n_out_rows):
    n_blocks = block_expert.shape[0]
    dh = wg.shape[3]
    any_spec = pl.BlockSpec(memory_space=pl.ANY)
    row_buf = pltpu.VMEM((MOE_BLOCK * ROW_TILES, LANES), jnp.uint32)
    grid_spec = pltpu.PrefetchScalarGridSpec(
        num_scalar_prefetch=3,
        grid=(n_blocks,),
        in_specs=[any_spec] * 5,
        out_specs=any_spec,
        scratch_shapes=[
            pltpu.SMEM((IDX_SLOTS, 2, MOE_BLOCK), jnp.int32),
            row_buf, row_buf, row_buf, row_buf,
            pltpu.VMEM((D, dh), F32), pltpu.VMEM((D, dh), F32), pltpu.VMEM((dh, D), F32),
            pltpu.VMEM((D, dh), BF16), pltpu.VMEM((D, dh), BF16), pltpu.VMEM((dh, D), BF16),
            pltpu.SemaphoreType.DMA((IDX_SLOTS,)), pltpu.SemaphoreType.DMA((2,)), pltpu.SemaphoreType.DMA((2,)),
            pltpu.SemaphoreType.DMA,
        ],
    )
    return pl.pallas_call(
        functools.partial(_routed_kernel, layer=layer),
        grid_spec=grid_spec,
        out_shape=jax.ShapeDtypeStruct((n_out_rows * ROW_TILES, LANES), jnp.uint32),
        compiler_params=_params(("arbitrary",)),
        name="routed_experts",
    )(block_expert, next_expert, n_used, idx_rows, x_packed, wg, wu, wd)


COMBINE_T = 128


def _combine_kernel(ys_ref, wt_ref, f_ref, wg_ref, wu_ref, wd_ref, h_ref, gate_ref, o_ref, wgb, wub, wdb):
    @pl.when(pl.program_id(0) == 0)
    def _():
        wgb[...] = wg_ref[...].astype(BF16)
        wub[...] = wu_ref[...].astype(BF16)
        wdb[...] = wd_ref[...].astype(BF16)

    shared = _swiglu(f_ref[...].astype(BF16), wgb[...], wub[...], wdb[...])
    wts = [wt_ref[:, k:k + 1] for k in range(TOP_K)]
    for s in range(ROW_TILES):
        c_lo = slice(s * LANES, (s + 1) * LANES)
        c_hi = slice(HALF + s * LANES, HALF + (s + 1) * LANES)
        acc_lo = shared[:, c_lo]
        acc_hi = shared[:, c_hi]
        for k in range(TOP_K):
            lo, hi = _load_packed(ys_ref, k, COMBINE_T, s, row_stride=TOP_K)
            acc_lo = acc_lo + wts[k] * lo
            acc_hi = acc_hi + wts[k] * hi
        o_ref[:, c_lo] = h_ref[:, c_lo] + gate_ref[:, c_lo] * acc_lo
        o_ref[:, c_hi] = h_ref[:, c_hi] + gate_ref[:, c_hi] * acc_hi


def combine(ys, wts, f, wsg, wsu, wsd, layer, h, mod3, gate_idx):
    per_tt = TT // COMBINE_T
    dh = wsg.shape[2]
    row = pl.BlockSpec((COMBINE_T, D), lambda i: (i, 0))
    return pl.pallas_call(
        _combine_kernel,
        grid=(N // COMBINE_T,),
        in_specs=[
            pl.BlockSpec((COMBINE_T * TOP_K * ROW_TILES, LANES), lambda i: (i, 0)),
            pl.BlockSpec((COMBINE_T, TOP_K), lambda i: (i, 0)),
            row,
            pl.BlockSpec((None, D, dh), lambda i: (layer, 0, 0)),
            pl.BlockSpec((None, D, dh), lambda i: (layer, 0, 0)),
            pl.BlockSpec((None, dh, D), lambda i: (layer, 0, 0)),
            row,
            pl.BlockSpec((None, 1, D), lambda i: (_mod_row(i // per_tt) * N_MOD + gate_idx, 0, 0)),
        ],
        out_specs=row,
        out_shape=jax.ShapeDtypeStruct((N, D), F32),
        scratch_shapes=[pltpu.VMEM((D, dh), BF16), pltpu.VMEM((D, dh), BF16), pltpu.VMEM((dh, D), BF16)],
        compiler_params=_params(("arbitrary",)),
        name="combine",
    )(ys, wts, f, wsg, wsu, wsd, h, mod3)


def moe_layer(h, f, f_packed, mod3, gate_idx, layer, w_router, b_router, wg, wu, wd, wsg, wsu, wsd):
    idx, wts, rank, counts = router(f, w_router, b_router)
    counts = counts[:, 0]
    padded = (counts + MOE_BLOCK - 1) // MOE_BLOCK * MOE_BLOCK
    pad_end = jnp.cumsum(padded)
    pad_start = pad_end - padded
    onehot = idx[:, :, None] == jnp.arange(N_EXPERTS, dtype=jnp.int32)[None, None, :]
    dest = jnp.sum(jnp.where(onehot, pad_start[None, None, :], 0), axis=-1) + rank
    n_blocks = -(-(N * TOP_K + N_EXPERTS * (MOE_BLOCK - 1)) // MOE_BLOCK)
    cap = n_blocks * MOE_BLOCK
    n_real = N * TOP_K
    e_ids = jnp.arange(N_EXPERTS, dtype=jnp.int32)
    block_start = jnp.arange(n_blocks, dtype=jnp.int32) * MOE_BLOCK
    block_expert = jnp.sum(block_start[:, None] >= pad_end[None, :], axis=1)
    block_expert = jnp.minimum(block_expert, N_EXPERTS - 1).astype(jnp.int32)
    n_used = (pad_end[-1:] // MOE_BLOCK).astype(jnp.int32)
    real_before = jnp.sum(jnp.where(block_expert[:, None] == e_ids[None, :], jnp.cumsum(counts)[None, :], 0),
                          axis=1)
    pos = jnp.arange(cap, dtype=jnp.int32)
    spare_row = n_real + pos - jnp.repeat(real_before, MOE_BLOCK)
    out_row = jnp.arange(N, dtype=jnp.int32)[None, :] * TOP_K + jnp.arange(TOP_K, dtype=jnp.int32)[:, None]
    out_rows = spare_row.at[dest.reshape(-1)].set(out_row.reshape(-1))
    src_tok = jnp.where(out_rows >= n_real, pos % N, out_rows // TOP_K)
    idx_rows = jnp.stack([src_tok.reshape(n_blocks, MOE_BLOCK), out_rows.reshape(n_blocks, MOE_BLOCK)], axis=1)
    last_rows = cap - MOE_BLOCK + jnp.arange(MOE_BLOCK, dtype=jnp.int32)
    idx_rows = jnp.concatenate([idx_rows, jnp.stack([last_rows % N, last_rows])[None]], axis=0)
    later = jnp.logical_and(e_ids[None, :] > e_ids[:, None], counts[None, :] > 0)
    next_by_expert = jnp.min(jnp.where(later, e_ids[None, :], N_EXPERTS), axis=1)
    next_by_expert = jnp.where(next_by_expert == N_EXPERTS, e_ids, next_by_expert)
    next_expert = jnp.sum(jnp.where(block_expert[:, None] == e_ids[None, :], next_by_expert[None, :], 0),
                          axis=1).astype(jnp.int32)
    ys = routed_experts(f_packed, idx_rows, block_expert, next_expert, n_used, wg, wu, wd, layer, cap)
    return combine(ys, wts.T, f, wsg, wsu, wsd, layer, h, mod3, gate_idx)


def _rope_tables(rot_dim):
    rows = T // GRID_W
    row = np.repeat(np.arange(rows, dtype=np.float32), GRID_W)
    col = np.tile(np.arange(GRID_W, dtype=np.float32), rows)
    half = rot_dim // 2
    inv_freq = jnp.asarray(ROPE_THETA, F32) ** (-jnp.arange(0, half, 2, dtype=F32) / half)
    ang_r = jnp.asarray(row)[:, None] * inv_freq[None, :]
    ang_c = jnp.asarray(col)[:, None] * inv_freq[None, :]
    ang = jnp.concatenate([ang_r, ang_r, ang_c, ang_c], axis=-1)
    cos, sin = jnp.cos(ang), jnp.sin(ang)
    quarter = (np.arange(rot_dim) // (rot_dim // 4)) % 2
    s1 = jnp.where(quarter[None, :] == 0, -sin, 0.0)
    s2 = jnp.where(quarter[None, :] == 1, sin, 0.0)

    def full(tbl, fill):
        tbl = jnp.pad(tbl, ((0, 0), (0, LANES - rot_dim)), constant_values=fill)
        return jnp.concatenate([jnp.full((CTX, LANES), fill, F32), tbl], axis=0)

    return full(cos, 1.0), full(s1, 0.0), full(s2, 0.0)


def _pad_heads(w, n_heads, width, new_width):
    k = w.shape[0]
    w = w.reshape(k, n_heads, width)
    return jnp.pad(w, ((0, 0), (0, 0), (0, new_width - width))).reshape(k, n_heads * new_width)


def _pad_gain(g, new_width):
    return jnp.pad(g, (0, new_width - g.shape[0])).reshape(1, new_width)


def mla_mixer(a, tables, w_down, g_q, g_kv, w_uq, w_ukv, g_qn, g_kn, w_o_args):
    c, s1, s2 = tables
    w_dq = w_down[:, :MLA_Q_RANK]
    w_dkv = w_down[:, MLA_Q_RANK:MLA_Q_RANK + MLA_KV_RANK]
    w_dr = jnp.pad(w_down[:, MLA_Q_RANK + MLA_KV_RANK:], ((0, 0), (0, LANES - MLA_ROPE)))
    cq = matmul(a, w_dq, tn=MLA_Q_RANK, extra=(g_q.reshape(1, -1),), extra_specs=(_const_spec(MLA_Q_RANK),),
                out_shapes=[jax.ShapeDtypeStruct((N, MLA_Q_RANK), BF16)], out_specs=[_row_spec(MLA_Q_RANK)],
                epilogue=_epi_rmsnorm, chunk=None, name="mla_down_q")[0]
    ckv = matmul(a, w_dkv, tn=MLA_KV_RANK, extra=(g_kv.reshape(1, -1),),
                 extra_specs=(_const_spec(MLA_KV_RANK),),
                 out_shapes=[jax.ShapeDtypeStruct((N, MLA_KV_RANK), BF16)], out_specs=[_row_spec(MLA_KV_RANK)],
                 epilogue=_epi_rmsnorm, chunk=None, name="mla_down_kv")[0]
    k_rope = matmul(a, w_dr, tn=LANES, out_shapes=[jax.ShapeDtypeStruct((N, LANES), F32)],
                    out_specs=[_row_spec(LANES)], epilogue=_epi_plain, chunk=None, name="mla_down_rope")[0]

    tn = 1024
    scale = MLA_QK ** -0.5 * LOG2E
    w_uq_p = _pad_heads(w_uq, MLA_HEADS, MLA_QK, MLA_HW)
    q = matmul(cq, w_uq_p, tn=tn, extra=(_pad_gain(g_qn * scale, MLA_HW), c, s1, s2),
               extra_specs=(_const_spec(MLA_HW), _table_spec(), _table_spec(), _table_spec()),
               out_shapes=[jax.ShapeDtypeStruct((N, MLA_HEADS * MLA_HW), BF16)], out_specs=[_row_spec(tn)],
               epilogue=_epi_mla_q, name="mla_up_q")[0]
    k, v = matmul(ckv, w_ukv, tn=tn, extra=(k_rope, _pad_gain(g_kn, MLA_HW), c, s1, s2),
                  extra_specs=(pl.BlockSpec((TT, LANES), lambda n, i: (i, 0)), _const_spec(MLA_HW),
                               _table_spec(), _table_spec(), _table_spec()),
                  out_shapes=[jax.ShapeDtypeStruct((N, MLA_HEADS * MLA_HW), BF16),
                              jax.ShapeDtypeStruct((N, MLA_HEADS * MLA_V), BF16)],
                  out_specs=[_row_spec(tn), _row_spec(tn // 2)],
                  epilogue=_epi_mla_kv, name="mla_up_kv")
    heads_step = 4
    o = attention(q, k, v, n_steps=MLA_HEADS // heads_step, n_heads=heads_step, kv_shared=False,
                  dqk=MLA_HW, dv=MLA_V, with_ctx=True)
    return _out_proj(o, *w_o_args)


def gqa_mixer(a, tables, w_qkv, g_qn, g_kn, w_o_args):
    c, s1, s2 = tables
    tn = 512
    scale = GQA_HD ** -0.5 * LOG2E
    rope_specs = (_const_spec(GQA_HD), _table_spec(), _table_spec(), _table_spec())
    nq = GQA_HEADS * GQA_HD
    nk = GQA_KV_HEADS * GQA_HD
    q = matmul(a, w_qkv, tn=tn, n_cols=nq, extra=((g_qn * scale).reshape(1, -1), c, s1, s2),
               extra_specs=rope_specs, out_shapes=[jax.ShapeDtypeStruct((N, nq), BF16)],
               out_specs=[_row_spec(tn)], epilogue=_epi_head_rope,
               name="gqa_q")[0]
    k = matmul(a, w_qkv, tn=tn, col_off=nq // tn, n_cols=nk, extra=(g_kn.reshape(1, -1), c, s1, s2),
               extra_specs=rope_specs, out_shapes=[jax.ShapeDtypeStruct((N, nk), BF16)],
               out_specs=[_row_spec(tn)], epilogue=_epi_head_rope,
               name="gqa_k")[0]
    v = matmul(a, w_qkv, tn=tn, col_off=(nq + nk) // tn, n_cols=nk,
               out_shapes=[jax.ShapeDtypeStruct((N, nk), BF16)], out_specs=[_row_spec(tn)],
               epilogue=_epi_plain, name="gqa_v")[0]
    o = attention(q, k, v, n_steps=GQA_KV_HEADS, n_heads=GQA_REP, kv_shared=True, dqk=GQA_HD, dv=GQA_HD,
                  with_ctx=True)
    return _out_proj(o, *w_o_args)


def _out_proj(o, w_o, h, mod3, gate_idx):
    tn = 1024
    return matmul(o, w_o, tn=tn, extra=(h, mod3),
                  extra_specs=(_row_spec(tn),
                               pl.BlockSpec((None, 1, tn), lambda n, i: (_mod_row(i) * N_MOD + gate_idx, 0, n))),
                  out_shapes=[jax.ShapeDtypeStruct((N, D), F32)], out_specs=[_row_spec(tn)],
                  epilogue=_epi_resgate, name="out_proj")[0]


def kernel(x, c, ctx, c_ctx, ada_w, ada_b, norm1_g, norm2_g, mla_w_down, mla_g_q, mla_g_kv, mla_w_uq,
           mla_w_ukv, mla_g_qn, mla_g_kn, mla_w_o, gqa_w_qkv, gqa_g_qn, gqa_g_kn, gqa_w_o, moe_w_router,
           moe_b_router, moe_w_gate, moe_w_up, moe_w_down, moe_ws_gate, moe_ws_up, moe_ws_down):
    depth = ada_w.shape[0]
    h = jnp.concatenate([ctx, x], axis=1).reshape(N, D)
    c_rows = jnp.concatenate([c_ctx[None, :], c, jnp.zeros((16 - 1 - B, D), F32)], axis=0)
    mod = ada_table(c_rows, ada_w, ada_b)
    tables_a = _rope_tables(MLA_ROPE)
    tables_b = _rope_tables(GQA_HD)
    for i in range(depth):
        mod3 = mod[i].reshape(16 * N_MOD, 1, D)
        a = norm_mod(h, norm1_g[i], mod3, 0, 1, BF16)
        j = i // 2
        if i % 2 == 0:
            h = mla_mixer(a, tables_a, mla_w_down[j], mla_g_q[j], mla_g_kv[j], mla_w_uq[j], mla_w_ukv[j],
                          mla_g_qn[j], mla_g_kn[j], (mla_w_o[j], h, mod3, 2))
        else:
            h = gqa_mixer(a, tables_b, gqa_w_qkv[j], gqa_g_qn[j], gqa_g_kn[j], (gqa_w_o[j], h, mod3, 2))
        f, f_packed = norm_mod(h, norm2_g[i], mod3, 3, 4, F32, with_packed=True)
        h = moe_layer(h, f, f_packed, mod3, 5, i, moe_w_router[i], moe_b_router[i], moe_w_gate, moe_w_up,
                      moe_w_down, moe_ws_gate, moe_ws_up, moe_ws_down)
    return h.reshape(B, S, D)[:, CTX:, :]
```

```python
import functools
from typing import Callable, NamedTuple

import jax
import jax.numpy as jnp
import numpy as np
from jax import lax
from jax.experimental import pallas as pl
from jax.experimental.pallas import tpu as pltpu

F32 = jnp.float32
BF16 = jnp.bfloat16
HIGHEST = lax.Precision.HIGHEST

D = 2048
B = 8
T = 2048
CTX = 256
S = CTX + T
N = B * S
TT = 256
TILES_B = S // TT
GRID_W = 64
ROPE_THETA = 10000.0
EPS = 1e-6
N_MOD = 6

MLA_HEADS = 16
MLA_Q_RANK = 768
MLA_KV_RANK = 512
MLA_NOPE = 128
MLA_ROPE = 64
MLA_QK = MLA_NOPE + MLA_ROPE
MLA_V = 128
MLA_HW = 256

GQA_HEADS = 16
GQA_KV_HEADS = 4
GQA_REP = GQA_HEADS // GQA_KV_HEADS
GQA_HD = 128

N_EXPERTS = 64
N_GROUPS = 8
GROUP_SIZE = N_EXPERTS // N_GROUPS
TOPK_GROUPS = 4
TOP_K = 8
D_EXPERT = 512
ROUTED_SCALE = 2.5
MOE_BLOCK = 256

LANES = 128
LOG2E = 1.4426950408889634
VMEM_LIMIT = 56 * 1024 * 1024


def _params(sem, vmem=VMEM_LIMIT):
    return pltpu.CompilerParams(dimension_semantics=sem, vmem_limit_bytes=vmem)


def _mod_row(i):
    return jnp.where(i % TILES_B == 0, 0, i // TILES_B + 1)


class Rows(NamedTuple):
    n_rows: int
    src_tile: Callable
    mod_row: Callable


LAT_TILES = T // TT
ALL_ROWS = Rows(N, lambda i: i, _mod_row)
LATENT_ROWS = Rows(B * T, lambda i: i // LAT_TILES * TILES_B + 1 + i % LAT_TILES, lambda i: i // LAT_TILES + 1)


def _ada_kernel(c_ref, w_ref, b_ref, o_ref):
    a = c_ref[...]
    a = a * jax.nn.sigmoid(a)
    o_ref[...] = jnp.dot(a, w_ref[...], preferred_element_type=F32, precision=HIGHEST) + b_ref[...]


def ada_table(c_rows, ada_w, ada_b):
    depth = ada_w.shape[0]
    tn = 1024
    return pl.pallas_call(
        _ada_kernel,
        grid=(depth, N_MOD * D // tn),
        in_specs=[
            pl.BlockSpec((16, D), lambda l, n: (0, 0)),
            pl.BlockSpec((None, D, tn), lambda l, n: (l, 0, n)),
            pl.BlockSpec((None, 1, tn), lambda l, n: (l, 0, n)),
        ],
        out_specs=pl.BlockSpec((None, 16, tn), lambda l, n: (l, 0, n)),
        out_shape=jax.ShapeDtypeStruct((depth, 16, N_MOD * D), F32),
        compiler_params=_params(("parallel", "parallel")),
        name="ada_table",
    )(c_rows, ada_w, ada_b.reshape(depth, 1, N_MOD * D))


HALF = D // 2
ROW_TILES = HALF // LANES


def _store_packed(ref, y):
    rows = y.shape[0]
    for s in range(ROW_TILES):
        lo = y[:, s * LANES:(s + 1) * LANES]
        hi = y[:, HALF + s * LANES:HALF + (s + 1) * LANES]
        ref[pl.ds(s, rows, stride=ROW_TILES), :] = pltpu.pack_elementwise([lo, hi], packed_dtype=BF16)


def _load_packed(ref, base, rows, s):
    u = ref[pl.ds(base * ROW_TILES + s, rows, stride=ROW_TILES), :]
    lo = pltpu.unpack_elementwise(u, index=0, packed_dtype=BF16, unpacked_dtype=F32)
    hi = pltpu.unpack_elementwise(u, index=1, packed_dtype=BF16, unpacked_dtype=F32)
    return lo, hi


def _norm_mod_kernel(h_ref, g_ref, sh_ref, sc_ref, o_ref, *packed_ref):
    x = h_ref[...]
    ms = jnp.mean(x * x, axis=-1, keepdims=True)
    y = x * lax.rsqrt(ms + EPS) * g_ref[...]
    y = y * (1.0 + sc_ref[...]) + sh_ref[...]
    o_ref[...] = y.astype(o_ref.dtype)
    if packed_ref:
        _store_packed(packed_ref[0], y)


def norm_mod(h, gain, mod3, shift_idx, scale_idx, out_dtype, with_packed=False, rows=ALL_ROWS):
    n = rows.n_rows
    row = pl.BlockSpec((TT, D), lambda i: (i, 0))
    out_specs, out_shape = [row], [jax.ShapeDtypeStruct((n, D), out_dtype)]
    if with_packed:
        out_specs.append(pl.BlockSpec((TT * ROW_TILES, LANES), lambda i: (i, 0)))
        out_shape.append(jax.ShapeDtypeStruct((n * ROW_TILES, LANES), jnp.uint32))
    outs = pl.pallas_call(
        _norm_mod_kernel,
        grid=(n // TT,),
        in_specs=[
            row,
            pl.BlockSpec((1, D), lambda i: (0, 0)),
            pl.BlockSpec((None, 1, D), lambda i: (rows.mod_row(i) * N_MOD + shift_idx, 0, 0)),
            pl.BlockSpec((None, 1, D), lambda i: (rows.mod_row(i) * N_MOD + scale_idx, 0, 0)),
        ],
        out_specs=out_specs,
        out_shape=out_shape,
        compiler_params=_params(("parallel",)),
        name="norm_mod",
    )(h, gain.reshape(1, D), mod3, mod3)
    return outs if with_packed else outs[0]


def _mm_kernel(*refs, n_extra, n_out, epilogue, lag):
    a_ref, w_ref = refs[0], refs[1]
    extra = refs[2:2 + n_extra]
    outs = refs[2 + n_extra:2 + n_extra + n_out]
    wb_ref = refs[2 + n_extra + n_out]
    i = pl.program_id(1)

    @pl.when(i == 0)
    def _():
        wb_ref[...] = w_ref[...].astype(BF16)

    def product():
        return jnp.dot(a_ref[...].astype(BF16), wb_ref[...], preferred_element_type=F32)

    if not lag:
        epilogue(product(), extra, outs)
        return

    acc0, acc1 = refs[3 + n_extra + n_out:]

    @pl.when(i == 0)
    def _():
        acc1[...] = jnp.zeros_like(acc1)

    for parity, (cur, prev) in enumerate(((acc0, acc1), (acc1, acc0))):
        @pl.when(i % 2 == parity)
        def _():
            epilogue(prev[...], extra, outs)
            cur[...] = product()


def _lagged(i, lag):
    return jnp.maximum(i - lag, 0)


def matmul(a, w, *, tn, col_off=0, n_cols=None, extra=(), extra_specs=(), out_shapes, out_specs, epilogue,
           name, lag=0, n_rows=N):
    k = a.shape[1]
    n_cols = w.shape[1] if n_cols is None else n_cols
    n_tiles = n_rows // TT
    kern = functools.partial(_mm_kernel, n_extra=len(extra), n_out=len(out_shapes), epilogue=epilogue, lag=lag)
    acc = [pltpu.VMEM((TT, tn), F32)] * 2 if lag else []
    return pl.pallas_call(
        kern,
        grid=(n_cols // tn, n_tiles + lag),
        in_specs=[
            pl.BlockSpec((TT, k), lambda n, i: (jnp.minimum(i, n_tiles - 1), 0)),
            pl.BlockSpec((k, tn), lambda n, i: (0, n + col_off)),
            *extra_specs,
        ],
        out_specs=out_specs,
        out_shape=out_shapes,
        scratch_shapes=[pltpu.VMEM((k, tn), BF16), *acc],
        compiler_params=_params(("parallel", "arbitrary")),
        name=name,
    )(a, w, *extra)


def _epi_plain(acc, extra, outs):
    outs[0][...] = acc.astype(outs[0].dtype)


def _epi_rmsnorm(acc, extra, outs):
    (g_ref,) = extra
    ms = jnp.mean(acc * acc, axis=-1, keepdims=True)
    outs[0][...] = (acc * lax.rsqrt(ms + EPS) * g_ref[...]).astype(outs[0].dtype)


def _epi_resgate(acc, extra, outs):
    res_ref, gate_ref = extra
    outs[0][...] = res_ref[...] + gate_ref[...] * acc


def _rope(x, c, s1, s2, quarter):
    return x * c + pltpu.roll(x, LANES - quarter, 1) * s1 + pltpu.roll(x, quarter, 1) * s2


def _epi_head_rope(acc, extra, outs):
    g_ref, c_ref, s1_ref, s2_ref = extra
    c, s1, s2 = c_ref[...], s1_ref[...], s2_ref[...]
    for j in range(acc.shape[1] // LANES):
        x = acc[:, j * LANES:(j + 1) * LANES]
        ms = jnp.mean(x * x, axis=-1, keepdims=True)
        xn = x * lax.rsqrt(ms + EPS) * g_ref[...]
        outs[0][:, j * LANES:(j + 1) * LANES] = _rope(xn, c, s1, s2, GQA_HD // 4).astype(outs[0].dtype)


def _mla_head(nope, rope, g_ref, c, s1, s2):
    ss = jnp.sum(nope * nope, axis=-1, keepdims=True) + jnp.sum(rope * rope, axis=-1, keepdims=True)
    r = lax.rsqrt(ss * (1.0 / MLA_QK) + EPS)
    nope_n = nope * r * g_ref[:, :LANES]
    rope_n = _rope(rope * r * g_ref[:, LANES:], c, s1, s2, MLA_ROPE // 4)
    return nope_n, rope_n


def _epi_mla_q(acc, extra, outs):
    g_ref, c_ref, s1_ref, s2_ref = extra
    c, s1, s2 = c_ref[...], s1_ref[...], s2_ref[...]
    for j in range(acc.shape[1] // MLA_HW):
        nope = acc[:, j * MLA_HW:j * MLA_HW + LANES]
        rope = acc[:, j * MLA_HW + LANES:(j + 1) * MLA_HW]
        nope_n, rope_n = _mla_head(nope, rope, g_ref, c, s1, s2)
        dst = j * MLA_HW
        outs[0][:, dst:dst + LANES] = nope_n.astype(outs[0].dtype)
        outs[0][:, dst + LANES:dst + MLA_HW] = rope_n.astype(outs[0].dtype)


def _epi_mla_kv(acc, extra, outs):
    kr_ref, g_ref, c_ref, s1_ref, s2_ref = extra
    k_out, v_out = outs
    c, s1, s2 = c_ref[...], s1_ref[...], s2_ref[...]
    rope = kr_ref[...]
    for j in range(acc.shape[1] // MLA_HW):
        nope = acc[:, j * MLA_HW:j * MLA_HW + LANES]
        v = acc[:, j * MLA_HW + LANES:(j + 1) * MLA_HW]
        nope_n, rope_n = _mla_head(nope, rope, g_ref, c, s1, s2)
        dst = j * MLA_HW
        k_out[:, dst:dst + LANES] = nope_n.astype(k_out.dtype)
        k_out[:, dst + LANES:dst + MLA_HW] = rope_n.astype(k_out.dtype)
        v_out[:, dst // 2:dst // 2 + LANES] = v.astype(v_out.dtype)


def _row_spec(width, lag=0):
    return pl.BlockSpec((TT, width), lambda n, i: (_lagged(i, lag), n))


def _const_spec(width):
    return pl.BlockSpec((1, width), lambda n, i: (0, 0))


def _table_spec(lag=0):
    return pl.BlockSpec((TT, LANES), lambda n, i: (_lagged(i, lag) % TILES_B, 0))


KEY_CHUNK = 768


def _attn_kernel(q_ref, k_ref, v_ref, o_ref, *, n_heads, kv_shared, dqk, dv, with_ctx):
    def compute(nk):
        chunk = min(KEY_CHUNK, nk)
        n_chunks = nk // chunk

        def scores(h, c):
            kh = 0 if kv_shared else h
            q = q_ref[:, h * dqk:(h + 1) * dqk]
            k = k_ref[c * chunk:(c + 1) * chunk, kh * dqk:(kh + 1) * dqk]
            return lax.dot_general(q, k, (((1,), (1,)), ((), ())), preferred_element_type=F32)

        def row_max(s_chunks):
            m = jnp.max(s_chunks[0], axis=-1, keepdims=True)
            for s in s_chunks[1:]:
                m = jnp.maximum(m, jnp.max(s, axis=-1, keepdims=True))
            return m

        s_cur = [scores(0, c) for c in range(n_chunks)]
        for h in range(n_heads):
            vh = 0 if kv_shared else h
            m = row_max(s_cur)
            s_next = []
            l = None
            o = None
            for c in range(n_chunks):
                p = jnp.exp2(s_cur[c] - m)
                l_c = jnp.sum(p, axis=-1, keepdims=True)
                v = v_ref[c * chunk:(c + 1) * chunk, vh * dv:(vh + 1) * dv]
                o_c = jnp.dot(p.astype(BF16), v, preferred_element_type=F32)
                l = l_c if l is None else l + l_c
                o = o_c if o is None else o + o_c
                if h + 1 < n_heads:
                    s_next.append(scores(h + 1, c))
            o_ref[:, h * dv:(h + 1) * dv] = (o / l).astype(o_ref.dtype)
            s_cur = s_next

    if with_ctx:
        @pl.when(pl.program_id(2) == 0)
        def _():
            compute(CTX)

        @pl.when(pl.program_id(2) > 0)
        def _():
            compute(S)
    else:
        compute(S)


def attention(q, k, v, *, n_steps, n_heads, kv_shared, dqk, dv, with_ctx):
    first = 0 if with_ctx else 1
    q_tiles = TILES_B - first
    kv_heads = 1 if kv_shared else n_heads
    kern = functools.partial(_attn_kernel, n_heads=n_heads, kv_shared=kv_shared, dqk=dqk, dv=dv,
                             with_ctx=with_ctx)
    return pl.pallas_call(
        kern,
        grid=(B, n_steps, q_tiles),
        in_specs=[
            pl.BlockSpec((TT, n_heads * dqk), lambda b, g, i: (b * TILES_B + first + i, g)),
            pl.BlockSpec((S, kv_heads * dqk), lambda b, g, i: (b, g)),
            pl.BlockSpec((S, kv_heads * dv), lambda b, g, i: (b, g)),
        ],
        out_specs=pl.BlockSpec((TT, n_heads * dv), lambda b, g, i: (b * q_tiles + i, g)),
        out_shape=jax.ShapeDtypeStruct((B * q_tiles * TT, n_steps * n_heads * dv), BF16),
        compiler_params=_params(("parallel", "parallel", "arbitrary")),
        name="attention",
    )(q, k, v)


def _first_index(hit, iota, size):
    return jnp.min(jnp.where(hit, iota, float(size)), axis=0, keepdims=True)


def _router_kernel(f_ref, wr_ref, br_ref, idx_ref, wt_ref, rank_ref, cnt_ref, carry_ref):
    @pl.when(pl.program_id(0) == 0)
    def _():
        carry_ref[...] = jnp.zeros_like(carry_ref)

    neg = -jnp.inf
    logits = lax.dot_general(wr_ref[...], f_ref[...], (((1,), (1,)), ((), ())),
                             preferred_element_type=F32, precision=HIGHEST)
    scores = jax.nn.sigmoid(logits)
    biased = scores + br_ref[...]

    iota_m = lax.broadcasted_iota(jnp.int32, (GROUP_SIZE, TT), 0).astype(F32)
    groups = [biased[g * GROUP_SIZE:(g + 1) * GROUP_SIZE, :] for g in range(N_GROUPS)]
    gs_rows = []
    for blk in groups:
        m1 = jnp.max(blk, axis=0, keepdims=True)
        i1 = _first_index(blk == m1, iota_m, GROUP_SIZE)
        m2 = jnp.max(jnp.where(iota_m == i1, neg, blk), axis=0, keepdims=True)
        gs_rows.append(m1 + m2)
    gs = jnp.concatenate(gs_rows, axis=0)

    iota_g = lax.broadcasted_iota(jnp.int32, gs.shape, 0).astype(F32)
    sel = jnp.zeros(gs.shape, F32)
    cur = gs
    for _ in range(TOPK_GROUPS):
        m = jnp.max(cur, axis=0, keepdims=True)
        hit = iota_g == _first_index(cur == m, iota_g, N_GROUPS)
        sel = jnp.where(hit, 1.0, sel)
        cur = jnp.where(hit, neg, cur)

    cur = jnp.concatenate(
        [jnp.where(sel[g:g + 1, :] > 0.5, groups[g], neg) for g in range(N_GROUPS)], axis=0)
    iota_e = lax.broadcasted_iota(jnp.int32, cur.shape, 0).astype(F32)
    assigned = jnp.zeros(cur.shape, F32)
    w_rows, hits = [], []
    for k in range(TOP_K):
        m = jnp.max(cur, axis=0, keepdims=True)
        first = _first_index(cur == m, iota_e, N_EXPERTS)
        hit = iota_e == first
        idx_ref[k:k + 1, :] = first.astype(jnp.int32)
        w_rows.append(jnp.sum(jnp.where(hit, scores, 0.0), axis=0, keepdims=True))
        hits.append(hit)
        assigned = jnp.where(hit, 1.0, assigned)
        cur = jnp.where(hit, neg, cur)

    w_sum = w_rows[0]
    for k in range(1, TOP_K):
        w_sum = w_sum + w_rows[k]
    for k in range(TOP_K):
        wt_ref[k:k + 1, :] = w_rows[k] / w_sum * ROUTED_SCALE

    r_i = lax.broadcasted_iota(jnp.int32, (TT, TT), 0)
    c_i = lax.broadcasted_iota(jnp.int32, (TT, TT), 1)
    upper = jnp.where(r_i <= c_i, 1.0, 0.0).astype(BF16)
    incl = jnp.dot(assigned.astype(BF16), upper, preferred_element_type=F32)
    rank_e = carry_ref[...] + incl - assigned
    for k in range(TOP_K):
        rank_k = jnp.sum(jnp.where(hits[k], rank_e, 0.0), axis=0, keepdims=True)
        rank_ref[k:k + 1, :] = rank_k.astype(jnp.int32)
    carry = carry_ref[...] + jnp.sum(assigned, axis=1, keepdims=True)
    carry_ref[...] = carry
    cnt_ref[...] = carry.astype(jnp.int32)


def router(f, w_router, b_router):
    n = f.shape[0]
    tok = pl.BlockSpec((TOP_K, TT), lambda i: (0, i))
    return pl.pallas_call(
        _router_kernel,
        grid=(n // TT,),
        in_specs=[
            pl.BlockSpec((TT, D), lambda i: (i, 0)),
            pl.BlockSpec((N_EXPERTS, D), lambda i: (0, 0)),
            pl.BlockSpec((N_EXPERTS, 1), lambda i: (0, 0)),
        ],
        out_specs=[tok, tok, tok, pl.BlockSpec((N_EXPERTS, 1), lambda i: (0, 0))],
        out_shape=[
            jax.ShapeDtypeStruct((TOP_K, n), jnp.int32),
            jax.ShapeDtypeStruct((TOP_K, n), F32),
            jax.ShapeDtypeStruct((TOP_K, n), jnp.int32),
            jax.ShapeDtypeStruct((N_EXPERTS, 1), jnp.int32),
        ],
        scratch_shapes=[pltpu.VMEM((N_EXPERTS, 1), F32)],
        compiler_params=_params(("arbitrary",)),
        name="router",
    )(f, w_router.T, b_router.reshape(N_EXPERTS, 1))


def _swiglu(x, wg, wu, wd):
    g = jnp.dot(x, wg, preferred_element_type=F32)
    u = jnp.dot(x, wu, preferred_element_type=F32)
    mid = (g * jax.nn.sigmoid(g) * u).astype(BF16)
    return jnp.dot(mid, wd, preferred_element_type=F32)


def _packed_rows(ref, first_row, n_rows):
    start = first_row * ROW_TILES
    if not isinstance(start, int):
        start = pl.multiple_of(start, ROW_TILES)
    return ref.at[pl.ds(start, n_rows * ROW_TILES)]


IDX_SLOTS = 4


def _routed_kernel(be_ref, ne_ref, nu_ref, idx_hbm, x_hbm, wg_hbm, wu_hbm, wd_hbm, ys_hbm,
                   idx_smem, xbuf0, xbuf1, ybuf0, ybuf1, wgs, wus, wds, wgb, wub, wdb,
                   sem_i, sem_g, sem_s, sem_w, *, layer):
    j = pl.program_id(0)
    n_used = nu_ref[0]
    n_blocks = pl.num_programs(0)
    xbufs = (xbuf0, xbuf1)
    ybufs = (ybuf0, ybuf1)

    def idx_copy(blk):
        s = blk % IDX_SLOTS
        return pltpu.make_async_copy(idx_hbm.at[jnp.minimum(blk, n_blocks - 1)], idx_smem.at[s], sem_i.at[s])

    def weight_copies(e):
        return [pltpu.make_async_copy(src.at[layer, e], dst, sem_w)
                for src, dst in ((wg_hbm, wgs), (wu_hbm, wus), (wd_hbm, wds))]

    def gather_start(t, p, r):
        pltpu.make_async_copy(_packed_rows(x_hbm, t, 1), _packed_rows(xbufs[p], r, 1), sem_g.at[p]).start()

    def scatter_start(d, p, r):
        pltpu.make_async_copy(_packed_rows(ybufs[p], r, 1), _packed_rows(ys_hbm, d, 1), sem_s.at[p]).start()

    def gather_wait(p):
        pltpu.make_async_copy(_packed_rows(x_hbm, 0, MOE_BLOCK), _packed_rows(xbufs[p], 0, MOE_BLOCK),
                              sem_g.at[p]).wait()

    def scatter_wait(p):
        pltpu.make_async_copy(_packed_rows(ybufs[p], 0, MOE_BLOCK), _packed_rows(ys_hbm, 0, MOE_BLOCK),
                              sem_s.at[p]).wait()

    @pl.when(jnp.logical_and(j == 0, n_used > 0))
    def _():
        first = idx_copy(0)
        first.start()
        first.wait()

        def issue(r, carry):
            gather_start(idx_smem[0, 0, r], 0, r)
            return carry

        lax.fori_loop(0, MOE_BLOCK, issue, 0, unroll=8)
        idx_copy(1).start()
        spare = pltpu.make_async_copy(idx_hbm.at[n_blocks], idx_smem.at[IDX_SLOTS - 1], sem_i.at[IDX_SLOTS - 1])
        spare.start()
        spare.wait()
        ybuf1[...] = jnp.zeros_like(ybuf1)
        for cp in weight_copies(be_ref[0]):
            cp.start()

    @pl.when(j >= n_used)
    def _():
        ybuf0[...] = jnp.zeros_like(ybuf0)
        fill = pltpu.make_async_copy(ybuf0, _packed_rows(ys_hbm, j * MOE_BLOCK, MOE_BLOCK), sem_s.at[0])
        fill.start()
        fill.wait()

    @pl.when(j < n_used)
    def _():
        e = be_ref[j]

        @pl.when(jnp.logical_or(j == 0, e != be_ref[jnp.maximum(j - 1, 0)]))
        def _():
            for cp in weight_copies(e):
                cp.wait()
            wgb[...] = wgs[...].astype(BF16)
            wub[...] = wus[...].astype(BF16)
            wdb[...] = wds[...].astype(BF16)

            @pl.when(ne_ref[j] != e)
            def _():
                for cp in weight_copies(ne_ref[j]):
                    cp.start()

        for cur in (0, 1):
            nxt = 1 - cur

            @pl.when(j % 2 == cur)
            def _():
                @pl.when(j >= 1)
                def _():
                    scatter_wait(cur)

                idx_copy(j + 1).wait()
                gather_wait(cur)
                g_slot = (j + 1) % IDX_SLOTS
                s_slot = (j + IDX_SLOTS - 1) % IDX_SLOTS
                for r in range(MOE_BLOCK):
                    gather_start(idx_smem[g_slot, 0, r], nxt, r)
                    scatter_start(idx_smem[s_slot, 1, r], nxt, r)
                pieces = [_load_packed(xbufs[cur], 0, MOE_BLOCK, s) for s in range(ROW_TILES)]
                x = jnp.concatenate([p[0].astype(BF16) for p in pieces] + [p[1].astype(BF16) for p in pieces],
                                    axis=1)
                _store_packed(ybufs[cur], _swiglu(x, wgb[...], wub[...], wdb[...]))
                idx_copy(j + 2).start()

                @pl.when(j == n_used - 1)
                def _():
                    def issue(r, carry):
                        scatter_start(idx_smem[j % IDX_SLOTS, 1, r], cur, r)
                        return carry

                    lax.fori_loop(0, MOE_BLOCK, issue, 0, unroll=8)
                    scatter_wait(nxt)
                    scatter_wait(cur)
                    gather_wait(nxt)
                    idx_copy(j + 2).wait()


def routed_experts(x_packed, idx_rows, block_expert, next_expert, n_used, wg, wu, wd, layer, n_out_rows):
    n_blocks = block_expert.shape[0]
    dh = wg.shape[3]
    any_spec = pl.BlockSpec(memory_space=pl.ANY)
    row_buf = pltpu.VMEM((MOE_BLOCK * ROW_TILES, LANES), jnp.uint32)
    grid_spec = pltpu.PrefetchScalarGridSpec(
        num_scalar_prefetch=3,
        grid=(n_blocks,),
        in_specs=[any_spec] * 5,
        out_specs=any_spec,
        scratch_shapes=[
            pltpu.SMEM((IDX_SLOTS, 2, MOE_BLOCK), jnp.int32),
            row_buf, row_buf, row_buf, row_buf,
            pltpu.VMEM((D, dh), F32), pltpu.VMEM((D, dh), F32), pltpu.VMEM((dh, D), F32),
            pltpu.VMEM((D, dh), BF16), pltpu.VMEM((D, dh), BF16), pltpu.VMEM((dh, D), BF16),
            pltpu.SemaphoreType.DMA((IDX_SLOTS,)), pltpu.SemaphoreType.DMA((2,)), pltpu.SemaphoreType.DMA((2,)),
            pltpu.SemaphoreType.DMA,
        ],
    )
    return pl.pallas_call(
        functools.partial(_routed_kernel, layer=layer),
        grid_spec=grid_spec,
        out_shape=jax.ShapeDtypeStruct((n_out_rows * ROW_TILES, LANES), jnp.uint32),
        compiler_params=_params(("arbitrary",)),
        name="routed_experts",
    )(block_expert, next_expert, n_used, idx_rows, x_packed, wg, wu, wd)


COMBINE_T = 128


def _combine_kernel(ys_ref, wt_ref, f_ref, wg_ref, wu_ref, wd_ref, h_ref, gate_ref, o_ref, wgb, wub, wdb):
    @pl.when(pl.program_id(0) == 0)
    def _():
        wgb[...] = wg_ref[...].astype(BF16)
        wub[...] = wu_ref[...].astype(BF16)
        wdb[...] = wd_ref[...].astype(BF16)

    shared = _swiglu(f_ref[...].astype(BF16), wgb[...], wub[...], wdb[...])
    wts = [wt_ref[:, k:k + 1] for k in range(TOP_K)]
    for s in range(ROW_TILES):
        c_lo = slice(s * LANES, (s + 1) * LANES)
        c_hi = slice(HALF + s * LANES, HALF + (s + 1) * LANES)
        acc_lo = shared[:, c_lo]
        acc_hi = shared[:, c_hi]
        for k in range(TOP_K):
            lo, hi = _load_packed(ys_ref, k * COMBINE_T, COMBINE_T, s)
            acc_lo = acc_lo + wts[k] * lo
            acc_hi = acc_hi + wts[k] * hi
        o_ref[:, c_lo] = h_ref[:, c_lo] + gate_ref[:, c_lo] * acc_lo
        o_ref[:, c_hi] = h_ref[:, c_hi] + gate_ref[:, c_hi] * acc_hi


def combine(ys, wts, f, wsg, wsu, wsd, layer, h, mod3, gate_idx, rows):
    per_tt = TT // COMBINE_T
    dh = wsg.shape[2]
    row = pl.BlockSpec((COMBINE_T, D), lambda i: (i, 0))
    return pl.pallas_call(
        _combine_kernel,
        grid=(rows.n_rows // COMBINE_T,),
        in_specs=[
            pl.BlockSpec((COMBINE_T * TOP_K * ROW_TILES, LANES), lambda i: (i, 0)),
            pl.BlockSpec((COMBINE_T, TOP_K), lambda i: (i, 0)),
            row,
            pl.BlockSpec((None, D, dh), lambda i: (layer, 0, 0)),
            pl.BlockSpec((None, D, dh), lambda i: (layer, 0, 0)),
            pl.BlockSpec((None, dh, D), lambda i: (layer, 0, 0)),
            row,
            pl.BlockSpec((None, 1, D), lambda i: (rows.mod_row(i // per_tt) * N_MOD + gate_idx, 0, 0)),
        ],
        out_specs=row,
        out_shape=jax.ShapeDtypeStruct((rows.n_rows, D), F32),
        scratch_shapes=[pltpu.VMEM((D, dh), BF16), pltpu.VMEM((D, dh), BF16), pltpu.VMEM((dh, D), BF16)],
        compiler_params=_params(("arbitrary",)),
        name="combine",
    )(ys, wts, f, wsg, wsu, wsd, h, mod3)


def moe_layer(h, f, f_packed, mod3, gate_idx, layer, rows, w_router, b_router, wg, wu, wd, wsg, wsu, wsd):
    n = rows.n_rows
    idx, wts, rank, counts = router(f, w_router, b_router)
    counts = counts[:, 0]
    padded = (counts + MOE_BLOCK - 1) // MOE_BLOCK * MOE_BLOCK
    pad_end = jnp.cumsum(padded)
    pad_start = pad_end - padded
    onehot = idx[:, :, None] == jnp.arange(N_EXPERTS, dtype=jnp.int32)[None, None, :]
    dest = jnp.sum(jnp.where(onehot, pad_start[None, None, :], 0), axis=-1) + rank
    n_blocks = -(-(n * TOP_K + N_EXPERTS * (MOE_BLOCK - 1)) // MOE_BLOCK)
    cap = n_blocks * MOE_BLOCK
    n_real = n * TOP_K
    e_ids = jnp.arange(N_EXPERTS, dtype=jnp.int32)
    block_start = jnp.arange(n_blocks, dtype=jnp.int32) * MOE_BLOCK
    block_expert = jnp.sum(block_start[:, None] >= pad_end[None, :], axis=1)
    block_expert = jnp.minimum(block_expert, N_EXPERTS - 1).astype(jnp.int32)
    n_used = (pad_end[-1:] // MOE_BLOCK).astype(jnp.int32)
    real_before = jnp.sum(jnp.where(block_expert[:, None] == e_ids[None, :], jnp.cumsum(counts)[None, :], 0),
                          axis=1)
    pos = jnp.arange(cap, dtype=jnp.int32)
    spare_row = n_real + pos - jnp.repeat(real_before, MOE_BLOCK)
    tile_rows = TOP_K * COMBINE_T
    tok = jnp.arange(n, dtype=jnp.int32)[None, :]
    out_row = tok // COMBINE_T * tile_rows + jnp.arange(TOP_K, dtype=jnp.int32)[:, None] * COMBINE_T + tok % COMBINE_T
    out_rows = spare_row.at[dest.reshape(-1)].set(out_row.reshape(-1))
    src_tok = jnp.where(out_rows >= n_real, pos % n, out_rows // tile_rows * COMBINE_T + out_rows % COMBINE_T)
    idx_rows = jnp.stack([src_tok.reshape(n_blocks, MOE_BLOCK), out_rows.reshape(n_blocks, MOE_BLOCK)], axis=1)
    last_rows = cap - MOE_BLOCK + jnp.arange(MOE_BLOCK, dtype=jnp.int32)
    idx_rows = jnp.concatenate([idx_rows, jnp.stack([last_rows % n, last_rows])[None]], axis=0)
    later = jnp.logical_and(e_ids[None, :] > e_ids[:, None], counts[None, :] > 0)
    next_by_expert = jnp.min(jnp.where(later, e_ids[None, :], N_EXPERTS), axis=1)
    next_by_expert = jnp.where(next_by_expert == N_EXPERTS, e_ids, next_by_expert)
    next_expert = jnp.sum(jnp.where(block_expert[:, None] == e_ids[None, :], next_by_expert[None, :], 0),
                          axis=1).astype(jnp.int32)
    ys = routed_experts(f_packed, idx_rows, block_expert, next_expert, n_used, wg, wu, wd, layer, cap)
    return combine(ys, wts.T, f, wsg, wsu, wsd, layer, h, mod3, gate_idx, rows)


def _rope_tables(rot_dim):
    rows = T // GRID_W
    row = np.repeat(np.arange(rows, dtype=np.float32), GRID_W)
    col = np.tile(np.arange(GRID_W, dtype=np.float32), rows)
    half = rot_dim // 2
    inv_freq = jnp.asarray(ROPE_THETA, F32) ** (-jnp.arange(0, half, 2, dtype=F32) / half)
    ang_r = jnp.asarray(row)[:, None] * inv_freq[None, :]
    ang_c = jnp.asarray(col)[:, None] * inv_freq[None, :]
    ang = jnp.concatenate([ang_r, ang_r, ang_c, ang_c], axis=-1)
    cos, sin = jnp.cos(ang), jnp.sin(ang)
    quarter = (np.arange(rot_dim) // (rot_dim // 4)) % 2
    s1 = jnp.where(quarter[None, :] == 0, -sin, 0.0)
    s2 = jnp.where(quarter[None, :] == 1, sin, 0.0)

    def full(tbl, fill):
        tbl = jnp.pad(tbl, ((0, 0), (0, LANES - rot_dim)), constant_values=fill)
        return jnp.concatenate([jnp.full((CTX, LANES), fill, F32), tbl], axis=0)

    return full(cos, 1.0), full(s1, 0.0), full(s2, 0.0)


def _pad_heads(w, n_heads, width, new_width):
    k = w.shape[0]
    w = w.reshape(k, n_heads, width)
    return jnp.pad(w, ((0, 0), (0, 0), (0, new_width - width))).reshape(k, n_heads * new_width)


def _pad_gain(g, new_width):
    return jnp.pad(g, (0, new_width - g.shape[0])).reshape(1, new_width)


def mla_mixer(a, tables, w_down, g_q, g_kv, w_uq, w_ukv, g_qn, g_kn, w_o_args, with_ctx):
    c, s1, s2 = tables
    w_dq = w_down[:, :MLA_Q_RANK]
    w_dkv = w_down[:, MLA_Q_RANK:MLA_Q_RANK + MLA_KV_RANK]
    w_dr = jnp.pad(w_down[:, MLA_Q_RANK + MLA_KV_RANK:], ((0, 0), (0, LANES - MLA_ROPE)))
    cq = matmul(a, w_dq, tn=MLA_Q_RANK, extra=(g_q.reshape(1, -1),), extra_specs=(_const_spec(MLA_Q_RANK),),
                out_shapes=[jax.ShapeDtypeStruct((N, MLA_Q_RANK), BF16)], out_specs=[_row_spec(MLA_Q_RANK)],
                epilogue=_epi_rmsnorm, name="mla_down_q")[0]
    ckv = matmul(a, w_dkv, tn=MLA_KV_RANK, extra=(g_kv.reshape(1, -1),),
                 extra_specs=(_const_spec(MLA_KV_RANK),),
                 out_shapes=[jax.ShapeDtypeStruct((N, MLA_KV_RANK), BF16)], out_specs=[_row_spec(MLA_KV_RANK)],
                 epilogue=_epi_rmsnorm, name="mla_down_kv")[0]
    k_rope = matmul(a, w_dr, tn=LANES, out_shapes=[jax.ShapeDtypeStruct((N, LANES), F32)],
                    out_specs=[_row_spec(LANES)], epilogue=_epi_plain, name="mla_down_rope")[0]

    tn = 1024
    scale = MLA_QK ** -0.5 * LOG2E
    w_uq_p = _pad_heads(w_uq, MLA_HEADS, MLA_QK, MLA_HW)
    rope_specs = (_const_spec(MLA_HW), _table_spec(1), _table_spec(1), _table_spec(1))
    q = matmul(cq, w_uq_p, tn=tn, extra=(_pad_gain(g_qn * scale, MLA_HW), c, s1, s2), extra_specs=rope_specs,
               out_shapes=[jax.ShapeDtypeStruct((N, MLA_HEADS * MLA_HW), BF16)], out_specs=[_row_spec(tn, 1)],
               epilogue=_epi_mla_q, lag=1, name="mla_up_q")[0]
    k, v = matmul(ckv, w_ukv, tn=tn, extra=(k_rope, _pad_gain(g_kn, MLA_HW), c, s1, s2),
                  extra_specs=(pl.BlockSpec((TT, LANES), lambda n, i: (_lagged(i, 1), 0)), *rope_specs),
                  out_shapes=[jax.ShapeDtypeStruct((N, MLA_HEADS * MLA_HW), BF16),
                              jax.ShapeDtypeStruct((N, MLA_HEADS * MLA_V), BF16)],
                  out_specs=[_row_spec(tn, 1), _row_spec(tn // 2, 1)],
                  epilogue=_epi_mla_kv, lag=1, name="mla_up_kv")
    heads_step = 4
    o = attention(q, k, v, n_steps=MLA_HEADS // heads_step, n_heads=heads_step, kv_shared=False,
                  dqk=MLA_HW, dv=MLA_V, with_ctx=with_ctx)
    return _out_proj(o, *w_o_args)


def gqa_mixer(a, tables, w_qkv, g_qn, g_kn, w_o_args, with_ctx):
    c, s1, s2 = tables
    tn = 512
    scale = GQA_HD ** -0.5 * LOG2E
    rope_specs = (_const_spec(GQA_HD), _table_spec(1), _table_spec(1), _table_spec(1))
    nq = GQA_HEADS * GQA_HD
    nk = GQA_KV_HEADS * GQA_HD
    q = matmul(a, w_qkv, tn=tn, n_cols=nq, extra=((g_qn * scale).reshape(1, -1), c, s1, s2),
               extra_specs=rope_specs, out_shapes=[jax.ShapeDtypeStruct((N, nq), BF16)],
               out_specs=[_row_spec(tn, 1)], epilogue=_epi_head_rope, lag=1, name="gqa_q")[0]
    k = matmul(a, w_qkv, tn=tn, col_off=nq // tn, n_cols=nk, extra=(g_kn.reshape(1, -1), c, s1, s2),
               extra_specs=rope_specs, out_shapes=[jax.ShapeDtypeStruct((N, nk), BF16)],
               out_specs=[_row_spec(tn, 1)], epilogue=_epi_head_rope, lag=1, name="gqa_k")[0]
    v = matmul(a, w_qkv, tn=tn, col_off=(nq + nk) // tn, n_cols=nk,
               out_shapes=[jax.ShapeDtypeStruct((N, nk), BF16)], out_specs=[_row_spec(tn)],
               epilogue=_epi_plain, name="gqa_v")[0]
    o = attention(q, k, v, n_steps=GQA_KV_HEADS, n_heads=GQA_REP, kv_shared=True, dqk=GQA_HD, dv=GQA_HD,
                  with_ctx=with_ctx)
    return _out_proj(o, *w_o_args)


def _out_proj(o, w_o, h, mod3, gate_idx, rows):
    tn = 1024
    return matmul(o, w_o, tn=tn, extra=(h, mod3),
                  extra_specs=(pl.BlockSpec((TT, tn), lambda n, i: (rows.src_tile(i), n)),
                               pl.BlockSpec((None, 1, tn), lambda n, i: (rows.mod_row(i) * N_MOD + gate_idx, 0, n))),
                  out_shapes=[jax.ShapeDtypeStruct((rows.n_rows, D), F32)], out_specs=[_row_spec(tn)],
                  epilogue=_epi_resgate, n_rows=rows.n_rows, name="out_proj")[0]


def kernel(x, c, ctx, c_ctx, ada_w, ada_b, norm1_g, norm2_g, mla_w_down, mla_g_q, mla_g_kv, mla_w_uq,
           mla_w_ukv, mla_g_qn, mla_g_kn, mla_w_o, gqa_w_qkv, gqa_g_qn, gqa_g_kn, gqa_w_o, moe_w_router,
           moe_b_router, moe_w_gate, moe_w_up, moe_w_down, moe_ws_gate, moe_ws_up, moe_ws_down):
    depth = ada_w.shape[0]
    h = jnp.concatenate([ctx, x], axis=1).reshape(N, D)
    c_rows = jnp.concatenate([c_ctx[None, :], c, jnp.zeros((16 - 1 - B, D), F32)], axis=0)
    mod = ada_table(c_rows, ada_w, ada_b)
    tables_a = _rope_tables(MLA_ROPE)
    tables_b = _rope_tables(GQA_HD)
    for i in range(depth):
        with_ctx = i < depth - 1
        rows = ALL_ROWS if with_ctx else LATENT_ROWS
        mod3 = mod[i].reshape(16 * N_MOD, 1, D)
        a = norm_mod(h, norm1_g[i], mod3, 0, 1, BF16)
        j = i // 2
        if i % 2 == 0:
            h = mla_mixer(a, tables_a, mla_w_down[j], mla_g_q[j], mla_g_kv[j], mla_w_uq[j], mla_w_ukv[j],
                          mla_g_qn[j], mla_g_kn[j], (mla_w_o[j], h, mod3, 2, rows), with_ctx)
        else:
            h = gqa_mixer(a, tables_b, gqa_w_qkv[j], gqa_g_qn[j], gqa_g_kn[j], (gqa_w_o[j], h, mod3, 2, rows),
                          with_ctx)
        f, f_packed = norm_mod(h, norm2_g[i], mod3, 3, 4, F32, with_packed=True, rows=rows)
        h = moe_layer(h, f, f_packed, mod3, 5, i, rows, moe_w_router[i], moe_b_router[i], moe_w_gate, moe_w_up,
                      moe_w_down, moe_ws_gate, moe_ws_up, moe_ws_down)
    return h.reshape(B, T, D)
```

```python
import functools
from typing import Callable, NamedTuple

import jax
import jax.numpy as jnp
import numpy as np
from jax import lax
from jax.experimental import pallas as pl
from jax.experimental.pallas import tpu as pltpu

F32 = jnp.float32
BF16 = jnp.bfloat16
HIGHEST = lax.Precision.HIGHEST

D = 2048
B = 8
T = 2048
CTX = 256
S = CTX + T
N = B * S
TT = 256
TILES_B = S // TT
GRID_W = 64
ROPE_THETA = 10000.0
EPS = 1e-6
N_MOD = 6

MLA_HEADS = 16
MLA_Q_RANK = 768
MLA_KV_RANK = 512
MLA_NOPE = 128
MLA_ROPE = 64
MLA_QK = MLA_NOPE + MLA_ROPE
MLA_V = 128
MLA_HW = 256

GQA_HEADS = 16
GQA_KV_HEADS = 4
GQA_REP = GQA_HEADS // GQA_KV_HEADS
GQA_HD = 128

N_EXPERTS = 64
N_GROUPS = 8
GROUP_SIZE = N_EXPERTS // N_GROUPS
TOPK_GROUPS = 4
TOP_K = 8
D_EXPERT = 512
ROUTED_SCALE = 2.5
MOE_BLOCK = 256

LANES = 128
LOG2E = 1.4426950408889634
VMEM_LIMIT = 56 * 1024 * 1024


def _params(sem, vmem=VMEM_LIMIT):
    return pltpu.CompilerParams(dimension_semantics=sem, vmem_limit_bytes=vmem)


def _mod_row(i):
    return jnp.where(i % TILES_B == 0, 0, i // TILES_B + 1)


class Rows(NamedTuple):
    n_rows: int
    src_tile: Callable
    mod_row: Callable


LAT_TILES = T // TT
ALL_ROWS = Rows(N, lambda i: i, _mod_row)
LATENT_ROWS = Rows(B * T, lambda i: i // LAT_TILES * TILES_B + 1 + i % LAT_TILES, lambda i: i // LAT_TILES + 1)


def _ada_kernel(c_ref, w_ref, b_ref, o_ref):
    a = c_ref[...]
    a = a * jax.nn.sigmoid(a)
    o_ref[...] = jnp.dot(a, w_ref[...], preferred_element_type=F32, precision=HIGHEST) + b_ref[...]


def ada_table(c_rows, ada_w, ada_b):
    depth = ada_w.shape[0]
    tn = 1024
    return pl.pallas_call(
        _ada_kernel,
        grid=(depth, N_MOD * D // tn),
        in_specs=[
            pl.BlockSpec((16, D), lambda l, n: (0, 0)),
            pl.BlockSpec((None, D, tn), lambda l, n: (l, 0, n)),
            pl.BlockSpec((None, 1, tn), lambda l, n: (l, 0, n)),
        ],
        out_specs=pl.BlockSpec((None, 16, tn), lambda l, n: (l, 0, n)),
        out_shape=jax.ShapeDtypeStruct((depth, 16, N_MOD * D), F32),
        compiler_params=_params(("parallel", "parallel")),
        name="ada_table",
    )(c_rows, ada_w, ada_b.reshape(depth, 1, N_MOD * D))


HALF = D // 2
ROW_TILES = HALF // LANES


def _store_packed(ref, y):
    rows = y.shape[0]
    for s in range(ROW_TILES):
        lo = y[:, s * LANES:(s + 1) * LANES]
        hi = y[:, HALF + s * LANES:HALF + (s + 1) * LANES]
        ref[pl.ds(s, rows, stride=ROW_TILES), :] = pltpu.pack_elementwise([lo, hi], packed_dtype=BF16)


def _load_packed(ref, base, rows, s):
    u = ref[pl.ds(base * ROW_TILES + s, rows, stride=ROW_TILES), :]
    lo = pltpu.unpack_elementwise(u, index=0, packed_dtype=BF16, unpacked_dtype=F32)
    hi = pltpu.unpack_elementwise(u, index=1, packed_dtype=BF16, unpacked_dtype=F32)
    return lo, hi


def _norm_mod_kernel(h_ref, g_ref, sh_ref, sc_ref, o_ref, *packed_ref):
    x = h_ref[...]
    ms = jnp.mean(x * x, axis=-1, keepdims=True)
    y = x * lax.rsqrt(ms + EPS) * g_ref[...]
    y = y * (1.0 + sc_ref[...]) + sh_ref[...]
    o_ref[...] = y.astype(o_ref.dtype)
    if packed_ref:
        _store_packed(packed_ref[0], y)


def norm_mod(h, gain, mod3, shift_idx, scale_idx, out_dtype, with_packed=False, rows=ALL_ROWS):
    n = rows.n_rows
    row = pl.BlockSpec((TT, D), lambda i: (i, 0))
    out_specs, out_shape = [row], [jax.ShapeDtypeStruct((n, D), out_dtype)]
    if with_packed:
        out_specs.append(pl.BlockSpec((TT * ROW_TILES, LANES), lambda i: (i, 0)))
        out_shape.append(jax.ShapeDtypeStruct((n * ROW_TILES, LANES), jnp.uint32))
    outs = pl.pallas_call(
        _norm_mod_kernel,
        grid=(n // TT,),
        in_specs=[
            row,
            pl.BlockSpec((1, D), lambda i: (0, 0)),
            pl.BlockSpec((None, 1, D), lambda i: (rows.mod_row(i) * N_MOD + shift_idx, 0, 0)),
            pl.BlockSpec((None, 1, D), lambda i: (rows.mod_row(i) * N_MOD + scale_idx, 0, 0)),
        ],
        out_specs=out_specs,
        out_shape=out_shape,
        compiler_params=_params(("parallel",)),
        name="norm_mod",
    )(h, gain.reshape(1, D), mod3, mod3)
    return outs if with_packed else outs[0]


def _mm_kernel(*refs, n_extra, n_out, epilogue, lag):
    a_ref, w_ref = refs[0], refs[1]
    extra = refs[2:2 + n_extra]
    outs = refs[2 + n_extra:2 + n_extra + n_out]
    wb_ref = refs[2 + n_extra + n_out]
    i = pl.program_id(1)

    @pl.when(i == 0)
    def _():
        wb_ref[...] = w_ref[...].astype(BF16)

    def product():
        return jnp.dot(a_ref[...].astype(BF16), wb_ref[...], preferred_element_type=F32)

    if not lag:
        epilogue(product(), extra, outs)
        return

    acc0, acc1 = refs[3 + n_extra + n_out:]

    @pl.when(i == 0)
    def _():
        acc1[...] = jnp.zeros_like(acc1)

    for parity, (cur, prev) in enumerate(((acc0, acc1), (acc1, acc0))):
        @pl.when(i % 2 == parity)
        def _():
            epilogue(prev[...], extra, outs)
            cur[...] = product()


def _lagged(i, lag):
    return jnp.maximum(i - lag, 0)


def matmul(a, w, *, tn, col_off=0, n_cols=None, extra=(), extra_specs=(), out_shapes, out_specs, epilogue,
           name, lag=0, n_rows=N):
    k = a.shape[1]
    n_cols = w.shape[1] if n_cols is None else n_cols
    n_tiles = n_rows // TT
    kern = functools.partial(_mm_kernel, n_extra=len(extra), n_out=len(out_shapes), epilogue=epilogue, lag=lag)
    acc = [pltpu.VMEM((TT, tn), F32)] * 2 if lag else []
    return pl.pallas_call(
        kern,
        grid=(n_cols // tn, n_tiles + lag),
        in_specs=[
            pl.BlockSpec((TT, k), lambda n, i: (jnp.minimum(i, n_tiles - 1), 0)),
            pl.BlockSpec((k, tn), lambda n, i: (0, n + col_off)),
            *extra_specs,
        ],
        out_specs=out_specs,
        out_shape=out_shapes,
        scratch_shapes=[pltpu.VMEM((k, tn), BF16), *acc],
        compiler_params=_params(("parallel", "arbitrary")),
        name=name,
    )(a, w, *extra)


def _epi_plain(acc, extra, outs):
    outs[0][...] = acc.astype(outs[0].dtype)


def _rms(x, g_ref):
    return x * lax.rsqrt(jnp.mean(x * x, axis=-1, keepdims=True) + EPS) * g_ref[...]


def _epi_mla_down(acc, extra, outs):
    gq_ref, gkv_ref = extra
    cq_out, ckv_out, rope_out = outs
    kv_end = MLA_Q_RANK + MLA_KV_RANK
    cq_out[...] = _rms(acc[:, :MLA_Q_RANK], gq_ref).astype(cq_out.dtype)
    ckv_out[...] = _rms(acc[:, MLA_Q_RANK:kv_end], gkv_ref).astype(ckv_out.dtype)
    rope_out[...] = acc[:, kv_end:]


def _epi_resgate(acc, extra, outs):
    res_ref, gate_ref = extra
    outs[0][...] = res_ref[...] + gate_ref[...] * acc


def _rope(x, c, s1, s2, quarter):
    return x * c + pltpu.roll(x, LANES - quarter, 1) * s1 + pltpu.roll(x, quarter, 1) * s2


def _epi_head_rope(acc, extra, outs):
    g_ref, c_ref, s1_ref, s2_ref = extra
    c, s1, s2 = c_ref[...], s1_ref[...], s2_ref[...]
    for j in range(acc.shape[1] // LANES):
        x = acc[:, j * LANES:(j + 1) * LANES]
        ms = jnp.mean(x * x, axis=-1, keepdims=True)
        xn = x * lax.rsqrt(ms + EPS) * g_ref[...]
        outs[0][:, j * LANES:(j + 1) * LANES] = _rope(xn, c, s1, s2, GQA_HD // 4).astype(outs[0].dtype)


def _mla_head(nope, rope, g_ref, c, s1, s2):
    ss = jnp.sum(nope * nope, axis=-1, keepdims=True) + jnp.sum(rope * rope, axis=-1, keepdims=True)
    r = lax.rsqrt(ss * (1.0 / MLA_QK) + EPS)
    nope_n = nope * r * g_ref[:, :LANES]
    rope_n = _rope(rope * r * g_ref[:, LANES:], c, s1, s2, MLA_ROPE // 4)
    return nope_n, rope_n


def _epi_mla_q(acc, extra, outs):
    g_ref, c_ref, s1_ref, s2_ref = extra
    c, s1, s2 = c_ref[...], s1_ref[...], s2_ref[...]
    for j in range(acc.shape[1] // MLA_HW):
        nope = acc[:, j * MLA_HW:j * MLA_HW + LANES]
        rope = acc[:, j * MLA_HW + LANES:(j + 1) * MLA_HW]
        nope_n, rope_n = _mla_head(nope, rope, g_ref, c, s1, s2)
        dst = j * MLA_HW
        outs[0][:, dst:dst + LANES] = nope_n.astype(outs[0].dtype)
        outs[0][:, dst + LANES:dst + MLA_HW] = rope_n.astype(outs[0].dtype)


def _epi_mla_kv(acc, extra, outs):
    kr_ref, g_ref, c_ref, s1_ref, s2_ref = extra
    k_out, v_out = outs
    c, s1, s2 = c_ref[...], s1_ref[...], s2_ref[...]
    rope = kr_ref[...]
    for j in range(acc.shape[1] // MLA_HW):
        nope = acc[:, j * MLA_HW:j * MLA_HW + LANES]
        v = acc[:, j * MLA_HW + LANES:(j + 1) * MLA_HW]
        nope_n, rope_n = _mla_head(nope, rope, g_ref, c, s1, s2)
        dst = j * MLA_HW
        k_out[:, dst:dst + LANES] = nope_n.astype(k_out.dtype)
        k_out[:, dst + LANES:dst + MLA_HW] = rope_n.astype(k_out.dtype)
        v_out[:, dst // 2:dst // 2 + LANES] = v.astype(v_out.dtype)


def _row_spec(width, lag=0):
    return pl.BlockSpec((TT, width), lambda n, i: (_lagged(i, lag), n))


def _const_spec(width):
    return pl.BlockSpec((1, width), lambda n, i: (0, 0))


def _table_spec(lag=0):
    return pl.BlockSpec((TT, LANES), lambda n, i: (_lagged(i, lag) % TILES_B, 0))


KEY_CHUNK = 768


def _attn_kernel(q_ref, k_ref, v_ref, o_ref, *, n_heads, kv_shared, dqk, dv, with_ctx):
    def compute(nk):
        chunk = min(KEY_CHUNK, nk)
        n_chunks = nk // chunk

        def scores(h, c):
            kh = 0 if kv_shared else h
            q = q_ref[:, h * dqk:(h + 1) * dqk]
            k = k_ref[c * chunk:(c + 1) * chunk, kh * dqk:(kh + 1) * dqk]
            return lax.dot_general(q, k, (((1,), (1,)), ((), ())), preferred_element_type=F32)

        def row_max(s_chunks):
            m = jnp.max(s_chunks[0], axis=-1, keepdims=True)
            for s in s_chunks[1:]:
                m = jnp.maximum(m, jnp.max(s, axis=-1, keepdims=True))
            return m

        s_cur = [scores(0, c) for c in range(n_chunks)]
        for h in range(n_heads):
            vh = 0 if kv_shared else h
            m = row_max(s_cur)
            s_next = []
            l = None
            o = None
            for c in range(n_chunks):
                p = jnp.exp2(s_cur[c] - m)
                l_c = jnp.sum(p, axis=-1, keepdims=True)
                v = v_ref[c * chunk:(c + 1) * chunk, vh * dv:(vh + 1) * dv]
                o_c = jnp.dot(p.astype(BF16), v, preferred_element_type=F32)
                l = l_c if l is None else l + l_c
                o = o_c if o is None else o + o_c
                if h + 1 < n_heads:
                    s_next.append(scores(h + 1, c))
            o_ref[:, h * dv:(h + 1) * dv] = (o / l).astype(o_ref.dtype)
            s_cur = s_next

    if with_ctx:
        @pl.when(pl.program_id(2) == 0)
        def _():
            compute(CTX)

        @pl.when(pl.program_id(2) > 0)
        def _():
            compute(S)
    else:
        compute(S)


def attention(q, k, v, *, n_steps, n_heads, kv_shared, dqk, dv, with_ctx):
    first = 0 if with_ctx else 1
    q_tiles = TILES_B - first
    kv_heads = 1 if kv_shared else n_heads
    kern = functools.partial(_attn_kernel, n_heads=n_heads, kv_shared=kv_shared, dqk=dqk, dv=dv,
                             with_ctx=with_ctx)
    return pl.pallas_call(
        kern,
        grid=(B, n_steps, q_tiles),
        in_specs=[
            pl.BlockSpec((TT, n_heads * dqk), lambda b, g, i: (b * TILES_B + first + i, g)),
            pl.BlockSpec((S, kv_heads * dqk), lambda b, g, i: (b, g)),
            pl.BlockSpec((S, kv_heads * dv), lambda b, g, i: (b, g)),
        ],
        out_specs=pl.BlockSpec((TT, n_heads * dv), lambda b, g, i: (b * q_tiles + i, g)),
        out_shape=jax.ShapeDtypeStruct((B * q_tiles * TT, n_steps * n_heads * dv), BF16),
        compiler_params=_params(("parallel", "parallel", "arbitrary")),
        name="attention",
    )(q, k, v)


def _first_index(hit, iota, size):
    return jnp.min(jnp.where(hit, iota, float(size)), axis=0, keepdims=True)


def _router_kernel(f_ref, wr_ref, br_ref, idx_ref, wt_ref, rank_ref, cnt_ref, carry_ref):
    @pl.when(pl.program_id(0) == 0)
    def _():
        carry_ref[...] = jnp.zeros_like(carry_ref)

    neg = -jnp.inf
    logits = lax.dot_general(wr_ref[...], f_ref[...], (((1,), (1,)), ((), ())),
                             preferred_element_type=F32, precision=HIGHEST)
    scores = jax.nn.sigmoid(logits)
    biased = scores + br_ref[...]

    iota_m = lax.broadcasted_iota(jnp.int32, (GROUP_SIZE, TT), 0).astype(F32)
    groups = [biased[g * GROUP_SIZE:(g + 1) * GROUP_SIZE, :] for g in range(N_GROUPS)]
    gs_rows = []
    for blk in groups:
        m1 = jnp.max(blk, axis=0, keepdims=True)
        i1 = _first_index(blk == m1, iota_m, GROUP_SIZE)
        m2 = jnp.max(jnp.where(iota_m == i1, neg, blk), axis=0, keepdims=True)
        gs_rows.append(m1 + m2)
    gs = jnp.concatenate(gs_rows, axis=0)

    iota_g = lax.broadcasted_iota(jnp.int32, gs.shape, 0).astype(F32)
    sel = jnp.zeros(gs.shape, F32)
    cur = gs
    for _ in range(TOPK_GROUPS):
        m = jnp.max(cur, axis=0, keepdims=True)
        hit = iota_g == _first_index(cur == m, iota_g, N_GROUPS)
        sel = jnp.where(hit, 1.0, sel)
        cur = jnp.where(hit, neg, cur)

    cur = jnp.concatenate(
        [jnp.where(sel[g:g + 1, :] > 0.5, groups[g], neg) for g in range(N_GROUPS)], axis=0)
    iota_e = lax.broadcasted_iota(jnp.int32, cur.shape, 0).astype(F32)
    assigned = jnp.zeros(cur.shape, F32)
    w_rows, hits = [], []
    for k in range(TOP_K):
        m = jnp.max(cur, axis=0, keepdims=True)
        first = _first_index(cur == m, iota_e, N_EXPERTS)
        hit = iota_e == first
        idx_ref[k:k + 1, :] = first.astype(jnp.int32)
        w_rows.append(jnp.sum(jnp.where(hit, scores, 0.0), axis=0, keepdims=True))
        hits.append(hit)
        assigned = jnp.where(hit, 1.0, assigned)
        cur = jnp.where(hit, neg, cur)

    w_sum = w_rows[0]
    for k in range(1, TOP_K):
        w_sum = w_sum + w_rows[k]
    for k in range(TOP_K):
        wt_ref[k:k + 1, :] = w_rows[k] / w_sum * ROUTED_SCALE

    r_i = lax.broadcasted_iota(jnp.int32, (TT, TT), 0)
    c_i = lax.broadcasted_iota(jnp.int32, (TT, TT), 1)
    upper = jnp.where(r_i <= c_i, 1.0, 0.0).astype(BF16)
    incl = jnp.dot(assigned.astype(BF16), upper, preferred_element_type=F32)
    rank_e = carry_ref[...] + incl - assigned
    for k in range(TOP_K):
        rank_k = jnp.sum(jnp.where(hits[k], rank_e, 0.0), axis=0, keepdims=True)
        rank_ref[k:k + 1, :] = rank_k.astype(jnp.int32)
    carry = carry_ref[...] + jnp.sum(assigned, axis=1, keepdims=True)
    carry_ref[...] = carry
    cnt_ref[...] = carry.astype(jnp.int32)


def router(f, w_router, b_router):
    n = f.shape[0]
    tok = pl.BlockSpec((TOP_K, TT), lambda i: (0, i))
    return pl.pallas_call(
        _router_kernel,
        grid=(n // TT,),
        in_specs=[
            pl.BlockSpec((TT, D), lambda i: (i, 0)),
            pl.BlockSpec((N_EXPERTS, D), lambda i: (0, 0)),
            pl.BlockSpec((N_EXPERTS, 1), lambda i: (0, 0)),
        ],
        out_specs=[tok, tok, tok, pl.BlockSpec((N_EXPERTS, 1), lambda i: (0, 0))],
        out_shape=[
            jax.ShapeDtypeStruct((TOP_K, n), jnp.int32),
            jax.ShapeDtypeStruct((TOP_K, n), F32),
            jax.ShapeDtypeStruct((TOP_K, n), jnp.int32),
            jax.ShapeDtypeStruct((N_EXPERTS, 1), jnp.int32),
        ],
        scratch_shapes=[pltpu.VMEM((N_EXPERTS, 1), F32)],
        compiler_params=_params(("arbitrary",)),
        name="router",
    )(f, w_router.T, b_router.reshape(N_EXPERTS, 1))


def _swiglu(x, wg, wu, wd):
    g = jnp.dot(x, wg, preferred_element_type=F32)
    u = jnp.dot(x, wu, preferred_element_type=F32)
    mid = (g * jax.nn.sigmoid(g) * u).astype(BF16)
    return jnp.dot(mid, wd, preferred_element_type=F32)


def _packed_rows(ref, first_row, n_rows):
    start = first_row * ROW_TILES
    if not isinstance(start, int):
        start = pl.multiple_of(start, ROW_TILES)
    return ref.at[pl.ds(start, n_rows * ROW_TILES)]


IDX_SLOTS = 4
WEIGHT_DMA_QUEUE = 1


def _routed_kernel(be_ref, ne_ref, nu_ref, idx_hbm, x_hbm, wg_hbm, wu_hbm, wd_hbm, ys_hbm,
                   idx_smem, xbuf0, xbuf1, ybuf0, ybuf1, wgs, wus, wds, wgb, wub, wdb,
                   sem_i, sem_g, sem_s, sem_w, *, layer):
    j = pl.program_id(0)
    n_used = nu_ref[0]
    n_blocks = pl.num_programs(0)
    xbufs = (xbuf0, xbuf1)
    ybufs = (ybuf0, ybuf1)

    def idx_copy(blk):
        s = blk % IDX_SLOTS
        return pltpu.make_async_copy(idx_hbm.at[jnp.minimum(blk, n_blocks - 1)], idx_smem.at[s], sem_i.at[s])

    def weight_copies(e):
        return [pltpu.make_async_copy(src.at[layer, e], dst, sem_w)
                for src, dst in ((wg_hbm, wgs), (wu_hbm, wus), (wd_hbm, wds))]

    def gather_start(t, p, r):
        pltpu.make_async_copy(_packed_rows(x_hbm, t, 1), _packed_rows(xbufs[p], r, 1), sem_g.at[p]).start()

    def scatter_start(d, p, r):
        pltpu.make_async_copy(_packed_rows(ybufs[p], r, 1), _packed_rows(ys_hbm, d, 1), sem_s.at[p]).start()

    def gather_wait(p):
        pltpu.make_async_copy(_packed_rows(x_hbm, 0, MOE_BLOCK), _packed_rows(xbufs[p], 0, MOE_BLOCK),
                              sem_g.at[p]).wait()

    def scatter_wait(p):
        pltpu.make_async_copy(_packed_rows(ybufs[p], 0, MOE_BLOCK), _packed_rows(ys_hbm, 0, MOE_BLOCK),
                              sem_s.at[p]).wait()

    @pl.when(jnp.logical_and(j == 0, n_used > 0))
    def _():
        first = idx_copy(0)
        first.start()
        first.wait()

        def issue(r, carry):
            gather_start(idx_smem[0, 0, r], 0, r)
            return carry

        lax.fori_loop(0, MOE_BLOCK, issue, 0, unroll=8)
        idx_copy(1).start()
        spare = pltpu.make_async_copy(idx_hbm.at[n_blocks], idx_smem.at[IDX_SLOTS - 1], sem_i.at[IDX_SLOTS - 1])
        spare.start()
        spare.wait()
        ybuf1[...] = jnp.zeros_like(ybuf1)
        for cp in weight_copies(be_ref[0]):
            cp.start(priority=WEIGHT_DMA_QUEUE)

    @pl.when(j >= n_used)
    def _():
        ybuf0[...] = jnp.zeros_like(ybuf0)
        fill = pltpu.make_async_copy(ybuf0, _packed_rows(ys_hbm, j * MOE_BLOCK, MOE_BLOCK), sem_s.at[0])
        fill.start()
        fill.wait()

    @pl.when(j < n_used)
    def _():
        e = be_ref[j]

        @pl.when(jnp.logical_or(j == 0, e != be_ref[jnp.maximum(j - 1, 0)]))
        def _():
            for cp in weight_copies(e):
                cp.wait()
            wgb[...] = wgs[...].astype(BF16)
            wub[...] = wus[...].astype(BF16)
            wdb[...] = wds[...].astype(BF16)

            @pl.when(ne_ref[j] != e)
            def _():
                for cp in weight_copies(ne_ref[j]):
                    cp.start(priority=WEIGHT_DMA_QUEUE)

        for cur in (0, 1):
            nxt = 1 - cur

            @pl.when(j % 2 == cur)
            def _():
                @pl.when(j >= 1)
                def _():
                    scatter_wait(cur)

                idx_copy(j + 1).wait()
                gather_wait(cur)
                g_slot = (j + 1) % IDX_SLOTS
                s_slot = (j + IDX_SLOTS - 1) % IDX_SLOTS
                for r in range(MOE_BLOCK):
                    gather_start(idx_smem[g_slot, 0, r], nxt, r)
                    scatter_start(idx_smem[s_slot, 1, r], nxt, r)
                pieces = [_load_packed(xbufs[cur], 0, MOE_BLOCK, s) for s in range(ROW_TILES)]
                x = jnp.concatenate([p[0].astype(BF16) for p in pieces] + [p[1].astype(BF16) for p in pieces],
                                    axis=1)
                _store_packed(ybufs[cur], _swiglu(x, wgb[...], wub[...], wdb[...]))
                idx_copy(j + 2).start()

                @pl.when(j == n_used - 1)
                def _():
                    def issue(r, carry):
                        scatter_start(idx_smem[j % IDX_SLOTS, 1, r], cur, r)
                        return carry

                    lax.fori_loop(0, MOE_BLOCK, issue, 0, unroll=8)
                    scatter_wait(nxt)
                    scatter_wait(cur)
                    gather_wait(nxt)
                    idx_copy(j + 2).wait()


def routed_experts(x_packed, idx_rows, block_expert, next_expert, n_used, wg, wu, wd, layer, n_out_rows):
    n_blocks = block_expert.shape[0]
    dh = wg.shape[3]
    any_spec = pl.BlockSpec(memory_space=pl.ANY)
    row_buf = pltpu.VMEM((MOE_BLOCK * ROW_TILES, LANES), jnp.uint32)
    grid_spec = pltpu.PrefetchScalarGridSpec(
        num_scalar_prefetch=3,
        grid=(n_blocks,),
        in_specs=[any_spec] * 5,
        out_specs=any_spec,
        scratch_shapes=[
            pltpu.SMEM((IDX_SLOTS, 2, MOE_BLOCK), jnp.int32),
            row_buf, row_buf, row_buf, row_buf,
            pltpu.VMEM((D, dh), F32), pltpu.VMEM((D, dh), F32), pltpu.VMEM((dh, D), F32),
            pltpu.VMEM((D, dh), BF16), pltpu.VMEM((D, dh), BF16), pltpu.VMEM((dh, D), BF16),
            pltpu.SemaphoreType.DMA((IDX_SLOTS,)), pltpu.SemaphoreType.DMA((2,)), pltpu.SemaphoreType.DMA((2,)),
            pltpu.SemaphoreType.DMA,
        ],
    )
    return pl.pallas_call(
        functools.partial(_routed_kernel, layer=layer),
        grid_spec=grid_spec,
        out_shape=jax.ShapeDtypeStruct((n_out_rows * ROW_TILES, LANES), jnp.uint32),
        compiler_params=_params(("arbitrary",)),
        name="routed_experts",
    )(block_expert, next_expert, n_used, idx_rows, x_packed, wg, wu, wd)


COMBINE_T = 128


def _combine_kernel(ys_ref, wt_ref, f_ref, wg_ref, wu_ref, wd_ref, h_ref, gate_ref, o_ref, wgb, wub, wdb):
    @pl.when(pl.program_id(0) == 0)
    def _():
        wgb[...] = wg_ref[...].astype(BF16)
        wub[...] = wu_ref[...].astype(BF16)
        wdb[...] = wd_ref[...].astype(BF16)

    shared = _swiglu(f_ref[...].astype(BF16), wgb[...], wub[...], wdb[...])
    wts = [wt_ref[:, k:k + 1] for k in range(TOP_K)]
    for s in range(ROW_TILES):
        c_lo = slice(s * LANES, (s + 1) * LANES)
        c_hi = slice(HALF + s * LANES, HALF + (s + 1) * LANES)
        acc_lo = shared[:, c_lo]
        acc_hi = shared[:, c_hi]
        for k in range(TOP_K):
            lo, hi = _load_packed(ys_ref, k * COMBINE_T, COMBINE_T, s)
            acc_lo = acc_lo + wts[k] * lo
            acc_hi = acc_hi + wts[k] * hi
        o_ref[:, c_lo] = h_ref[:, c_lo] + gate_ref[:, c_lo] * acc_lo
        o_ref[:, c_hi] = h_ref[:, c_hi] + gate_ref[:, c_hi] * acc_hi


def combine(ys, wts, f, wsg, wsu, wsd, layer, h, mod3, gate_idx, rows):
    per_tt = TT // COMBINE_T
    dh = wsg.shape[2]
    row = pl.BlockSpec((COMBINE_T, D), lambda i: (i, 0))
    return pl.pallas_call(
        _combine_kernel,
        grid=(rows.n_rows // COMBINE_T,),
        in_specs=[
            pl.BlockSpec((COMBINE_T * TOP_K * ROW_TILES, LANES), lambda i: (i, 0)),
            pl.BlockSpec((COMBINE_T, TOP_K), lambda i: (i, 0)),
            row,
            pl.BlockSpec((None, D, dh), lambda i: (layer, 0, 0)),
            pl.BlockSpec((None, D, dh), lambda i: (layer, 0, 0)),
            pl.BlockSpec((None, dh, D), lambda i: (layer, 0, 0)),
            row,
            pl.BlockSpec((None, 1, D), lambda i: (rows.mod_row(i // per_tt) * N_MOD + gate_idx, 0, 0)),
        ],
        out_specs=row,
        out_shape=jax.ShapeDtypeStruct((rows.n_rows, D), F32),
        scratch_shapes=[pltpu.VMEM((D, dh), BF16), pltpu.VMEM((D, dh), BF16), pltpu.VMEM((dh, D), BF16)],
        compiler_params=_params(("arbitrary",)),
        name="combine",
    )(ys, wts, f, wsg, wsu, wsd, h, mod3)


def moe_layer(h, f, f_packed, mod3, gate_idx, layer, rows, w_router, b_router, wg, wu, wd, wsg, wsu, wsd):
    n = rows.n_rows
    idx, wts, rank, counts = router(f, w_router, b_router)
    counts = counts[:, 0]
    padded = (counts + MOE_BLOCK - 1) // MOE_BLOCK * MOE_BLOCK
    pad_end = jnp.cumsum(padded)
    pad_start = pad_end - padded
    onehot = idx[:, :, None] == jnp.arange(N_EXPERTS, dtype=jnp.int32)[None, None, :]
    dest = jnp.sum(jnp.where(onehot, pad_start[None, None, :], 0), axis=-1) + rank
    n_blocks = -(-(n * TOP_K + N_EXPERTS * (MOE_BLOCK - 1)) // MOE_BLOCK)
    cap = n_blocks * MOE_BLOCK
    n_real = n * TOP_K
    e_ids = jnp.arange(N_EXPERTS, dtype=jnp.int32)
    block_start = jnp.arange(n_blocks, dtype=jnp.int32) * MOE_BLOCK
    block_expert = jnp.sum(block_start[:, None] >= pad_end[None, :], axis=1)
    block_expert = jnp.minimum(block_expert, N_EXPERTS - 1).astype(jnp.int32)
    n_used = (pad_end[-1:] // MOE_BLOCK).astype(jnp.int32)
    real_before = jnp.sum(jnp.where(block_expert[:, None] == e_ids[None, :], jnp.cumsum(counts)[None, :], 0),
                          axis=1)
    pos = jnp.arange(cap, dtype=jnp.int32)
    spare_row = n_real + pos - jnp.repeat(real_before, MOE_BLOCK)
    tile_rows = TOP_K * COMBINE_T
    tok = jnp.arange(n, dtype=jnp.int32)[None, :]
    out_row = tok // COMBINE_T * tile_rows + jnp.arange(TOP_K, dtype=jnp.int32)[:, None] * COMBINE_T + tok % COMBINE_T
    out_rows = spare_row.at[dest.reshape(-1)].set(out_row.reshape(-1))
    src_tok = jnp.where(out_rows >= n_real, pos % n, out_rows // tile_rows * COMBINE_T + out_rows % COMBINE_T)
    idx_rows = jnp.stack([src_tok.reshape(n_blocks, MOE_BLOCK), out_rows.reshape(n_blocks, MOE_BLOCK)], axis=1)
    last_rows = cap - MOE_BLOCK + jnp.arange(MOE_BLOCK, dtype=jnp.int32)
    idx_rows = jnp.concatenate([idx_rows, jnp.stack([last_rows % n, last_rows])[None]], axis=0)
    later = jnp.logical_and(e_ids[None, :] > e_ids[:, None], counts[None, :] > 0)
    next_by_expert = jnp.min(jnp.where(later, e_ids[None, :], N_EXPERTS), axis=1)
    next_by_expert = jnp.where(next_by_expert == N_EXPERTS, e_ids, next_by_expert)
    next_expert = jnp.sum(jnp.where(block_expert[:, None] == e_ids[None, :], next_by_expert[None, :], 0),
                          axis=1).astype(jnp.int32)
    ys = routed_experts(f_packed, idx_rows, block_expert, next_expert, n_used, wg, wu, wd, layer, cap)
    return combine(ys, wts.T, f, wsg, wsu, wsd, layer, h, mod3, gate_idx, rows)


def _rope_tables(rot_dim):
    rows = T // GRID_W
    row = np.repeat(np.arange(rows, dtype=np.float32), GRID_W)
    col = np.tile(np.arange(GRID_W, dtype=np.float32), rows)
    half = rot_dim // 2
    inv_freq = jnp.asarray(ROPE_THETA, F32) ** (-jnp.arange(0, half, 2, dtype=F32) / half)
    ang_r = jnp.asarray(row)[:, None] * inv_freq[None, :]
    ang_c = jnp.asarray(col)[:, None] * inv_freq[None, :]
    ang = jnp.concatenate([ang_r, ang_r, ang_c, ang_c], axis=-1)
    cos, sin = jnp.cos(ang), jnp.sin(ang)
    quarter = (np.arange(rot_dim) // (rot_dim // 4)) % 2
    s1 = jnp.where(quarter[None, :] == 0, -sin, 0.0)
    s2 = jnp.where(quarter[None, :] == 1, sin, 0.0)

    def full(tbl, fill):
        tbl = jnp.pad(tbl, ((0, 0), (0, LANES - rot_dim)), constant_values=fill)
        return jnp.concatenate([jnp.full((CTX, LANES), fill, F32), tbl], axis=0)

    return full(cos, 1.0), full(s1, 0.0), full(s2, 0.0)


def _pad_heads(w, n_heads, width, new_width):
    k = w.shape[0]
    w = w.reshape(k, n_heads, width)
    return jnp.pad(w, ((0, 0), (0, 0), (0, new_width - width))).reshape(k, n_heads * new_width)


def _pad_gain(g, new_width):
    return jnp.pad(g, (0, new_width - g.shape[0])).reshape(1, new_width)


def mla_mixer(a, tables, w_down, g_q, g_kv, w_uq, w_ukv, g_qn, g_kn, w_o_args, with_ctx):
    c, s1, s2 = tables
    w_down_p = jnp.pad(w_down, ((0, 0), (0, LANES - MLA_ROPE)))
    cq, ckv, k_rope = matmul(
        a, w_down_p, tn=w_down_p.shape[1], extra=(g_q.reshape(1, -1), g_kv.reshape(1, -1)),
        extra_specs=(_const_spec(MLA_Q_RANK), _const_spec(MLA_KV_RANK)),
        out_shapes=[jax.ShapeDtypeStruct((N, MLA_Q_RANK), BF16), jax.ShapeDtypeStruct((N, MLA_KV_RANK), BF16),
                    jax.ShapeDtypeStruct((N, LANES), F32)],
        out_specs=[_row_spec(MLA_Q_RANK), _row_spec(MLA_KV_RANK), _row_spec(LANES)],
        epilogue=_epi_mla_down, name="mla_down")

    tn = 1024
    scale = MLA_QK ** -0.5 * LOG2E
    w_uq_p = _pad_heads(w_uq, MLA_HEADS, MLA_QK, MLA_HW)
    rope_specs = (_const_spec(MLA_HW), _table_spec(1), _table_spec(1), _table_spec(1))
    q = matmul(cq, w_uq_p, tn=tn, extra=(_pad_gain(g_qn * scale, MLA_HW), c, s1, s2), extra_specs=rope_specs,
               out_shapes=[jax.ShapeDtypeStruct((N, MLA_HEADS * MLA_HW), BF16)], out_specs=[_row_spec(tn, 1)],
               epilogue=_epi_mla_q, lag=1, name="mla_up_q")[0]
    k, v = matmul(ckv, w_ukv, tn=tn, extra=(k_rope, _pad_gain(g_kn, MLA_HW), c, s1, s2),
                  extra_specs=(pl.BlockSpec((TT, LANES), lambda n, i: (_lagged(i, 1), 0)), *rope_specs),
                  out_shapes=[jax.ShapeDtypeStruct((N, MLA_HEADS * MLA_HW), BF16),
                              jax.ShapeDtypeStruct((N, MLA_HEADS * MLA_V), BF16)],
                  out_specs=[_row_spec(tn, 1), _row_spec(tn // 2, 1)],
                  epilogue=_epi_mla_kv, lag=1, name="mla_up_kv")
    heads_step = 4
    o = attention(q, k, v, n_steps=MLA_HEADS // heads_step, n_heads=heads_step, kv_shared=False,
                  dqk=MLA_HW, dv=MLA_V, with_ctx=with_ctx)
    return _out_proj(o, *w_o_args)


def gqa_mixer(a, tables, w_qkv, g_qn, g_kn, w_o_args, with_ctx):
    c, s1, s2 = tables
    tn = 512
    scale = GQA_HD ** -0.5 * LOG2E
    rope_specs = (_const_spec(GQA_HD), _table_spec(1), _table_spec(1), _table_spec(1))
    nq = GQA_HEADS * GQA_HD
    nk = GQA_KV_HEADS * GQA_HD
    q = matmul(a, w_qkv, tn=tn, n_cols=nq, extra=((g_qn * scale).reshape(1, -1), c, s1, s2),
               extra_specs=rope_specs, out_shapes=[jax.ShapeDtypeStruct((N, nq), BF16)],
               out_specs=[_row_spec(tn, 1)], epilogue=_epi_head_rope, lag=1, name="gqa_q")[0]
    k = matmul(a, w_qkv, tn=tn, col_off=nq // tn, n_cols=nk, extra=(g_kn.reshape(1, -1), c, s1, s2),
               extra_specs=rope_specs, out_shapes=[jax.ShapeDtypeStruct((N, nk), BF16)],
               out_specs=[_row_spec(tn, 1)], epilogue=_epi_head_rope, lag=1, name="gqa_k")[0]
    v = matmul(a, w_qkv, tn=tn, col_off=(nq + nk) // tn, n_cols=nk,
               out_shapes=[jax.ShapeDtypeStruct((N, nk), BF16)], out_specs=[_row_spec(tn)],
               epilogue=_epi_plain, name="gqa_v")[0]
    o = attention(q, k, v, n_steps=GQA_KV_HEADS, n_heads=GQA_REP, kv_shared=True, dqk=GQA_HD, dv=GQA_HD,
                  with_ctx=with_ctx)
    return _out_proj(o, *w_o_args)


def _out_proj(o, w_o, h, mod3, gate_idx, rows):
    tn = 1024
    return matmul(o, w_o, tn=tn, extra=(h, mod3),
                  extra_specs=(pl.BlockSpec((TT, tn), lambda n, i: (rows.src_tile(i), n)),
                               pl.BlockSpec((None, 1, tn), lambda n, i: (rows.mod_row(i) * N_MOD + gate_idx, 0, n))),
                  out_shapes=[jax.ShapeDtypeStruct((rows.n_rows, D), F32)], out_specs=[_row_spec(tn)],
                  epilogue=_epi_resgate, n_rows=rows.n_rows, name="out_proj")[0]


def kernel(x, c, ctx, c_ctx, ada_w, ada_b, norm1_g, norm2_g, mla_w_down, mla_g_q, mla_g_kv, mla_w_uq,
           mla_w_ukv, mla_g_qn, mla_g_kn, mla_w_o, gqa_w_qkv, gqa_g_qn, gqa_g_kn, gqa_w_o, moe_w_router,
           moe_b_router, moe_w_gate, moe_w_up, moe_w_down, moe_ws_gate, moe_ws_up, moe_ws_down):
    depth = ada_w.shape[0]
    h = jnp.concatenate([ctx, x], axis=1).reshape(N, D)
    c_rows = jnp.concatenate([c_ctx[None, :], c, jnp.zeros((16 - 1 - B, D), F32)], axis=0)
    mod = ada_table(c_rows, ada_w, ada_b)
    tables_a = _rope_tables(MLA_ROPE)
    tables_b = _rope_tables(GQA_HD)
    for i in range(depth):
        with_ctx = i < depth - 1
        rows = ALL_ROWS if with_ctx else LATENT_ROWS
        mod3 = mod[i].reshape(16 * N_MOD, 1, D)
        a = norm_mod(h, norm1_g[i], mod3, 0, 1, BF16)
        j = i // 2
        if i % 2 == 0:
            h = mla_mixer(a, tables_a, mla_w_down[j], mla_g_q[j], mla_g_kv[j], mla_w_uq[j], mla_w_ukv[j],
                          mla_g_qn[j], mla_g_kn[j], (mla_w_o[j], h, mod3, 2, rows), with_ctx)
        else:
            h = gqa_mixer(a, tables_b, gqa_w_qkv[j], gqa_g_qn[j], gqa_g_kn[j], (gqa_w_o[j], h, mod3, 2, rows),
                          with_ctx)
        f, f_packed = norm_mod(h, norm2_g[i], mod3, 3, 4, F32, with_packed=True, rows=rows)
        h = moe_layer(h, f, f_packed, mod3, 5, i, rows, moe_w_router[i], moe_b_router[i], moe_w_gate, moe_w_up,
                      moe_w_down, moe_ws_gate, moe_ws_up, moe_ws_down)
    return h.reshape(B, T, D)
```

```python
import functools
from typing import Callable, NamedTuple

import jax
import jax.numpy as jnp
import numpy as np
from jax import lax
from jax.experimental import pallas as pl
from jax.experimental.pallas import tpu as pltpu

F32 = jnp.float32
BF16 = jnp.bfloat16
HIGHEST = lax.Precision.HIGHEST

D = 2048
B = 8
T = 2048
CTX = 256
S = CTX + T
N = B * S
TT = 256
TILES_B = S // TT
GRID_W = 64
ROPE_THETA = 10000.0
EPS = 1e-6
N_MOD = 6

MLA_HEADS = 16
MLA_Q_RANK = 768
MLA_KV_RANK = 512
MLA_NOPE = 128
MLA_ROPE = 64
MLA_QK = MLA_NOPE + MLA_ROPE
MLA_V = 128
MLA_HW = 256

GQA_HEADS = 16
GQA_KV_HEADS = 4
GQA_REP = GQA_HEADS // GQA_KV_HEADS
GQA_HD = 128

N_EXPERTS = 64
N_GROUPS = 8
GROUP_SIZE = N_EXPERTS // N_GROUPS
TOPK_GROUPS = 4
TOP_K = 8
D_EXPERT = 512
ROUTED_SCALE = 2.5
MOE_BLOCK = 256

LANES = 128
LOG2E = 1.4426950408889634
VMEM_LIMIT = 56 * 1024 * 1024


def _params(sem, vmem=VMEM_LIMIT):
    return pltpu.CompilerParams(dimension_semantics=sem, vmem_limit_bytes=vmem)


def _mod_row(i):
    return jnp.where(i % TILES_B == 0, 0, i // TILES_B + 1)


class Rows(NamedTuple):
    n_rows: int
    src_tile: Callable
    mod_row: Callable


LAT_TILES = T // TT
ALL_ROWS = Rows(N, lambda i: i, _mod_row)
LATENT_ROWS = Rows(B * T, lambda i: i // LAT_TILES * TILES_B + 1 + i % LAT_TILES, lambda i: i // LAT_TILES + 1)


def _ada_kernel(c_ref, w_ref, b_ref, o_ref):
    a = c_ref[...]
    a = a * jax.nn.sigmoid(a)
    o_ref[...] = jnp.dot(a, w_ref[...], preferred_element_type=F32, precision=HIGHEST) + b_ref[...]


def ada_table(c_rows, ada_w, ada_b):
    depth = ada_w.shape[0]
    tn = 1024
    return pl.pallas_call(
        _ada_kernel,
        grid=(depth, N_MOD * D // tn),
        in_specs=[
            pl.BlockSpec((16, D), lambda l, n: (0, 0)),
            pl.BlockSpec((None, D, tn), lambda l, n: (l, 0, n)),
            pl.BlockSpec((None, 1, tn), lambda l, n: (l, 0, n)),
        ],
        out_specs=pl.BlockSpec((None, 16, tn), lambda l, n: (l, 0, n)),
        out_shape=jax.ShapeDtypeStruct((depth, 16, N_MOD * D), F32),
        compiler_params=_params(("parallel", "parallel")),
        name="ada_table",
    )(c_rows, ada_w, ada_b.reshape(depth, 1, N_MOD * D))


HALF = D // 2
ROW_TILES = HALF // LANES


def _store_packed(ref, y):
    rows = y.shape[0]
    for s in range(ROW_TILES):
        lo = y[:, s * LANES:(s + 1) * LANES]
        hi = y[:, HALF + s * LANES:HALF + (s + 1) * LANES]
        ref[pl.ds(s, rows, stride=ROW_TILES), :] = pltpu.pack_elementwise([lo, hi], packed_dtype=BF16)


def _load_packed(ref, base, rows, s):
    u = ref[pl.ds(base * ROW_TILES + s, rows, stride=ROW_TILES), :]
    lo = pltpu.unpack_elementwise(u, index=0, packed_dtype=BF16, unpacked_dtype=F32)
    hi = pltpu.unpack_elementwise(u, index=1, packed_dtype=BF16, unpacked_dtype=F32)
    return lo, hi


def _norm_mod_kernel(h_ref, g_ref, sh_ref, sc_ref, o_ref, *packed_ref):
    x = h_ref[...]
    ms = jnp.mean(x * x, axis=-1, keepdims=True)
    y = x * lax.rsqrt(ms + EPS) * g_ref[...]
    y = y * (1.0 + sc_ref[...]) + sh_ref[...]
    o_ref[...] = y.astype(o_ref.dtype)
    if packed_ref:
        _store_packed(packed_ref[0], y)


def norm_mod(h, gain, mod3, shift_idx, scale_idx, out_dtype, with_packed=False, rows=ALL_ROWS):
    n = rows.n_rows
    row = pl.BlockSpec((TT, D), lambda i: (i, 0))
    out_specs, out_shape = [row], [jax.ShapeDtypeStruct((n, D), out_dtype)]
    if with_packed:
        out_specs.append(pl.BlockSpec((TT * ROW_TILES, LANES), lambda i: (i, 0)))
        out_shape.append(jax.ShapeDtypeStruct((n * ROW_TILES, LANES), jnp.uint32))
    outs = pl.pallas_call(
        _norm_mod_kernel,
        grid=(n // TT,),
        in_specs=[
            row,
            pl.BlockSpec((1, D), lambda i: (0, 0)),
            pl.BlockSpec((None, 1, D), lambda i: (rows.mod_row(i) * N_MOD + shift_idx, 0, 0)),
            pl.BlockSpec((None, 1, D), lambda i: (rows.mod_row(i) * N_MOD + scale_idx, 0, 0)),
        ],
        out_specs=out_specs,
        out_shape=out_shape,
        compiler_params=_params(("parallel",)),
        name="norm_mod",
    )(h, gain.reshape(1, D), mod3, mod3)
    return outs if with_packed else outs[0]


def _mm_kernel(*refs, n_extra, n_out, epilogue, lag):
    a_ref, w_ref = refs[0], refs[1]
    extra = refs[2:2 + n_extra]
    outs = refs[2 + n_extra:2 + n_extra + n_out]
    wb_ref = refs[2 + n_extra + n_out]
    i = pl.program_id(1)

    @pl.when(i == 0)
    def _():
        wb_ref[...] = w_ref[...].astype(BF16)

    def product():
        return jnp.dot(a_ref[...].astype(BF16), wb_ref[...], preferred_element_type=F32)

    if not lag:
        epilogue(product(), extra, outs)
        return

    acc0, acc1 = refs[3 + n_extra + n_out:]

    @pl.when(i == 0)
    def _():
        acc1[...] = jnp.zeros_like(acc1)

    for parity, (cur, prev) in enumerate(((acc0, acc1), (acc1, acc0))):
        @pl.when(i % 2 == parity)
        def _():
            epilogue(prev[...], extra, outs)
            cur[...] = product()


def _lagged(i, lag):
    return jnp.maximum(i - lag, 0)


def matmul(a, w, *, tn, col_off=0, n_cols=None, extra=(), extra_specs=(), out_shapes, out_specs, epilogue,
           name, lag=0, n_rows=N):
    k = a.shape[1]
    n_cols = w.shape[1] if n_cols is None else n_cols
    n_tiles = n_rows // TT
    kern = functools.partial(_mm_kernel, n_extra=len(extra), n_out=len(out_shapes), epilogue=epilogue, lag=lag)
    acc = [pltpu.VMEM((TT, tn), F32)] * 2 if lag else []
    return pl.pallas_call(
        kern,
        grid=(n_cols // tn, n_tiles + lag),
        in_specs=[
            pl.BlockSpec((TT, k), lambda n, i: (jnp.minimum(i, n_tiles - 1), 0)),
            pl.BlockSpec((k, tn), lambda n, i: (0, n + col_off)),
            *extra_specs,
        ],
        out_specs=out_specs,
        out_shape=out_shapes,
        scratch_shapes=[pltpu.VMEM((k, tn), BF16), *acc],
        compiler_params=_params(("parallel", "arbitrary")),
        name=name,
    )(a, w, *extra)


def _epi_plain(acc, extra, outs):
    outs[0][...] = acc.astype(outs[0].dtype)


def _rms(x, g_ref):
    return x * lax.rsqrt(jnp.mean(x * x, axis=-1, keepdims=True) + EPS) * g_ref[...]


def _epi_mla_down(acc, extra, outs):
    gq_ref, gkv_ref = extra
    cq_out, ckv_out, rope_out = outs
    kv_end = MLA_Q_RANK + MLA_KV_RANK
    cq_out[...] = _rms(acc[:, :MLA_Q_RANK], gq_ref).astype(cq_out.dtype)
    ckv_out[...] = _rms(acc[:, MLA_Q_RANK:kv_end], gkv_ref).astype(ckv_out.dtype)
    rope_out[...] = acc[:, kv_end:]


def _epi_resgate(acc, extra, outs):
    res_ref, gate_ref = extra
    outs[0][...] = res_ref[...] + gate_ref[...] * acc


def _rope(x, c, s1, s2, quarter):
    return x * c + pltpu.roll(x, LANES - quarter, 1) * s1 + pltpu.roll(x, quarter, 1) * s2


def _epi_head_rope(acc, extra, outs):
    g_ref, c_ref, s1_ref, s2_ref = extra
    c, s1, s2 = c_ref[...], s1_ref[...], s2_ref[...]
    for j in range(acc.shape[1] // LANES):
        x = acc[:, j * LANES:(j + 1) * LANES]
        ms = jnp.mean(x * x, axis=-1, keepdims=True)
        xn = x * lax.rsqrt(ms + EPS) * g_ref[...]
        outs[0][:, j * LANES:(j + 1) * LANES] = _rope(xn, c, s1, s2, GQA_HD // 4).astype(outs[0].dtype)


def _mla_head(nope, rope, g_ref, c, s1, s2):
    ss = jnp.sum(nope * nope, axis=-1, keepdims=True) + jnp.sum(rope * rope, axis=-1, keepdims=True)
    r = lax.rsqrt(ss * (1.0 / MLA_QK) + EPS)
    nope_n = nope * r * g_ref[:, :LANES]
    rope_n = _rope(rope * r * g_ref[:, LANES:], c, s1, s2, MLA_ROPE // 4)
    return nope_n, rope_n


def _epi_mla_q(acc, extra, outs):
    g_ref, c_ref, s1_ref, s2_ref = extra
    c, s1, s2 = c_ref[...], s1_ref[...], s2_ref[...]
    for j in range(acc.shape[1] // MLA_HW):
        nope = acc[:, j * MLA_HW:j * MLA_HW + LANES]
        rope = acc[:, j * MLA_HW + LANES:(j + 1) * MLA_HW]
        nope_n, rope_n = _mla_head(nope, rope, g_ref, c, s1, s2)
        dst = j * MLA_HW
        outs[0][:, dst:dst + LANES] = nope_n.astype(outs[0].dtype)
        outs[0][:, dst + LANES:dst + MLA_HW] = rope_n.astype(outs[0].dtype)


def _epi_mla_kv(acc, extra, outs):
    kr_ref, g_ref, c_ref, s1_ref, s2_ref = extra
    k_out, v_out = outs
    c, s1, s2 = c_ref[...], s1_ref[...], s2_ref[...]
    rope = kr_ref[...]
    for j in range(acc.shape[1] // MLA_HW):
        nope = acc[:, j * MLA_HW:j * MLA_HW + LANES]
        v = acc[:, j * MLA_HW + LANES:(j + 1) * MLA_HW]
        nope_n, rope_n = _mla_head(nope, rope, g_ref, c, s1, s2)
        dst = j * MLA_HW
        k_out[:, dst:dst + LANES] = nope_n.astype(k_out.dtype)
        k_out[:, dst + LANES:dst + MLA_HW] = rope_n.astype(k_out.dtype)
        v_out[:, dst // 2:dst // 2 + LANES] = v.astype(v_out.dtype)


def _row_spec(width, lag=0):
    return pl.BlockSpec((TT, width), lambda n, i: (_lagged(i, lag), n))


def _const_spec(width):
    return pl.BlockSpec((1, width), lambda n, i: (0, 0))


def _table_spec(lag=0):
    return pl.BlockSpec((TT, LANES), lambda n, i: (_lagged(i, lag) % TILES_B, 0))


KEY_CHUNK = 768


def _attn_kernel(q_ref, k_ref, v_ref, o_ref, *, n_heads, kv_shared, dqk, dv, with_ctx):
    def compute(nk):
        chunk = min(KEY_CHUNK, nk)
        n_chunks = nk // chunk

        def scores(h, c):
            kh = 0 if kv_shared else h
            q = q_ref[:, h * dqk:(h + 1) * dqk]
            k = k_ref[c * chunk:(c + 1) * chunk, kh * dqk:(kh + 1) * dqk]
            return lax.dot_general(q, k, (((1,), (1,)), ((), ())), preferred_element_type=F32)

        def row_max(s_chunks):
            m = jnp.max(s_chunks[0], axis=-1, keepdims=True)
            for s in s_chunks[1:]:
                m = jnp.maximum(m, jnp.max(s, axis=-1, keepdims=True))
            return m

        s_cur = [scores(0, c) for c in range(n_chunks)]
        for h in range(n_heads):
            vh = 0 if kv_shared else h
            m = row_max(s_cur)
            s_next = []
            l = None
            o = None
            for c in range(n_chunks):
                p = jnp.exp2(s_cur[c] - m)
                l_c = jnp.sum(p, axis=-1, keepdims=True)
                v = v_ref[c * chunk:(c + 1) * chunk, vh * dv:(vh + 1) * dv]
                o_c = jnp.dot(p.astype(BF16), v, preferred_element_type=F32)
                l = l_c if l is None else l + l_c
                o = o_c if o is None else o + o_c
                if h + 1 < n_heads:
                    s_next.append(scores(h + 1, c))
            o_ref[:, h * dv:(h + 1) * dv] = (o / l).astype(o_ref.dtype)
            s_cur = s_next

    if with_ctx:
        @pl.when(pl.program_id(2) == 0)
        def _():
            compute(CTX)

        @pl.when(pl.program_id(2) > 0)
        def _():
            compute(S)
    else:
        compute(S)


def attention(q, k, v, *, n_steps, n_heads, kv_shared, dqk, dv, with_ctx):
    kv_heads = 1 if kv_shared else n_heads
    kern = functools.partial(_attn_kernel, n_heads=n_heads, kv_shared=kv_shared, dqk=dqk, dv=dv,
                             with_ctx=with_ctx)
    if with_ctx:
        q_rows, q_blocks, out_rows = TT, TILES_B, N
        q_spec = pl.BlockSpec((TT, n_heads * dqk), lambda b, g, i: (b * TILES_B + i, g))
    else:
        q_rows, q_blocks, out_rows = 2 * TT, T // (2 * TT), B * T
        q_cols = n_heads * dqk
        q_spec = pl.BlockSpec((pl.Element(q_rows), pl.Element(q_cols)),
                              lambda b, g, i: (pl.multiple_of(b * S + CTX + i * q_rows, TT),
                                               pl.multiple_of(g * q_cols, LANES)))
    return pl.pallas_call(
        kern,
        grid=(B, n_steps, q_blocks),
        in_specs=[
            q_spec,
            pl.BlockSpec((S, kv_heads * dqk), lambda b, g, i: (b, g)),
            pl.BlockSpec((S, kv_heads * dv), lambda b, g, i: (b, g)),
        ],
        out_specs=pl.BlockSpec((q_rows, n_heads * dv), lambda b, g, i: (b * q_blocks + i, g)),
        out_shape=jax.ShapeDtypeStruct((out_rows, n_steps * n_heads * dv), BF16),
        compiler_params=_params(("parallel", "parallel", "arbitrary")),
        name="attention",
    )(q, k, v)


def _first_index(hit, iota, size):
    return jnp.min(jnp.where(hit, iota, float(size)), axis=0, keepdims=True)


def _router_kernel(f_ref, wr_ref, br_ref, idx_ref, wt_ref, rank_ref, cnt_ref, carry_ref):
    @pl.when(pl.program_id(0) == 0)
    def _():
        carry_ref[...] = jnp.zeros_like(carry_ref)

    neg = -jnp.inf
    logits = lax.dot_general(wr_ref[...], f_ref[...], (((1,), (1,)), ((), ())),
                             preferred_element_type=F32, precision=HIGHEST)
    scores = jax.nn.sigmoid(logits)
    biased = scores + br_ref[...]

    iota_m = lax.broadcasted_iota(jnp.int32, (GROUP_SIZE, TT), 0).astype(F32)
    groups = [biased[g * GROUP_SIZE:(g + 1) * GROUP_SIZE, :] for g in range(N_GROUPS)]
    gs_rows = []
    for blk in groups:
        m1 = jnp.max(blk, axis=0, keepdims=True)
        i1 = _first_index(blk == m1, iota_m, GROUP_SIZE)
        m2 = jnp.max(jnp.where(iota_m == i1, neg, blk), axis=0, keepdims=True)
        gs_rows.append(m1 + m2)
    gs = jnp.concatenate(gs_rows, axis=0)

    iota_g = lax.broadcasted_iota(jnp.int32, gs.shape, 0).astype(F32)
    sel = jnp.zeros(gs.shape, F32)
    cur = gs
    for _ in range(TOPK_GROUPS):
        m = jnp.max(cur, axis=0, keepdims=True)
        hit = iota_g == _first_index(cur == m, iota_g, N_GROUPS)
        sel = jnp.where(hit, 1.0, sel)
        cur = jnp.where(hit, neg, cur)

    cur = jnp.concatenate(
        [jnp.where(sel[g:g + 1, :] > 0.5, groups[g], neg) for g in range(N_GROUPS)], axis=0)
    iota_e = lax.broadcasted_iota(jnp.int32, cur.shape, 0).astype(F32)
    assigned = jnp.zeros(cur.shape, F32)
    w_rows, hits = [], []
    for k in range(TOP_K):
        m = jnp.max(cur, axis=0, keepdims=True)
        first = _first_index(cur == m, iota_e, N_EXPERTS)
        hit = iota_e == first
        idx_ref[k:k + 1, :] = first.astype(jnp.int32)
        w_rows.append(jnp.sum(jnp.where(hit, scores, 0.0), axis=0, keepdims=True))
        hits.append(hit)
        assigned = jnp.where(hit, 1.0, assigned)
        cur = jnp.where(hit, neg, cur)

    w_sum = w_rows[0]
    for k in range(1, TOP_K):
        w_sum = w_sum + w_rows[k]
    for k in range(TOP_K):
        wt_ref[k:k + 1, :] = w_rows[k] / w_sum * ROUTED_SCALE

    r_i = lax.broadcasted_iota(jnp.int32, (TT, TT), 0)
    c_i = lax.broadcasted_iota(jnp.int32, (TT, TT), 1)
    upper = jnp.where(r_i <= c_i, 1.0, 0.0).astype(BF16)
    incl = jnp.dot(assigned.astype(BF16), upper, preferred_element_type=F32)
    rank_e = carry_ref[...] + incl - assigned
    for k in range(TOP_K):
        rank_k = jnp.sum(jnp.where(hits[k], rank_e, 0.0), axis=0, keepdims=True)
        rank_ref[k:k + 1, :] = rank_k.astype(jnp.int32)
    carry = carry_ref[...] + jnp.sum(assigned, axis=1, keepdims=True)
    carry_ref[...] = carry
    cnt_ref[...] = carry.astype(jnp.int32)


def router(f, w_router, b_router):
    n = f.shape[0]
    tok = pl.BlockSpec((TOP_K, TT), lambda i: (0, i))
    return pl.pallas_call(
        _router_kernel,
        grid=(n // TT,),
        in_specs=[
            pl.BlockSpec((TT, D), lambda i: (i, 0)),
            pl.BlockSpec((N_EXPERTS, D), lambda i: (0, 0)),
            pl.BlockSpec((N_EXPERTS, 1), lambda i: (0, 0)),
        ],
        out_specs=[tok, tok, tok, pl.BlockSpec((N_EXPERTS, 1), lambda i: (0, 0))],
        out_shape=[
            jax.ShapeDtypeStruct((TOP_K, n), jnp.int32),
            jax.ShapeDtypeStruct((TOP_K, n), F32),
            jax.ShapeDtypeStruct((TOP_K, n), jnp.int32),
            jax.ShapeDtypeStruct((N_EXPERTS, 1), jnp.int32),
        ],
        scratch_shapes=[pltpu.VMEM((N_EXPERTS, 1), F32)],
        compiler_params=_params(("arbitrary",)),
        name="router",
    )(f, w_router.T, b_router.reshape(N_EXPERTS, 1))


def _swiglu(x, wg, wu, wd):
    g = jnp.dot(x, wg, preferred_element_type=F32)
    u = jnp.dot(x, wu, preferred_element_type=F32)
    mid = (g * jax.nn.sigmoid(g) * u).astype(BF16)
    return jnp.dot(mid, wd, preferred_element_type=F32)


def _packed_rows(ref, first_row, n_rows):
    start = first_row * ROW_TILES
    if not isinstance(start, int):
        start = pl.multiple_of(start, ROW_TILES)
    return ref.at[pl.ds(start, n_rows * ROW_TILES)]


IDX_SLOTS = 4
WEIGHT_DMA_QUEUE = 1
SCATTER_DMA_QUEUE = 1


def _routed_kernel(be_ref, ne_ref, nu_ref, idx_hbm, x_hbm, wg_hbm, wu_hbm, wd_hbm, ys_hbm,
                   idx_smem, xbuf0, xbuf1, ybuf0, ybuf1, wgs, wus, wds, wgb, wub, wdb,
                   sem_i, sem_g, sem_s, sem_w, *, layer):
    j = pl.program_id(0)
    n_used = nu_ref[0]
    n_blocks = pl.num_programs(0)
    xbufs = (xbuf0, xbuf1)
    ybufs = (ybuf0, ybuf1)

    def idx_copy(blk):
        s = blk % IDX_SLOTS
        return pltpu.make_async_copy(idx_hbm.at[jnp.minimum(blk, n_blocks - 1)], idx_smem.at[s], sem_i.at[s])

    def weight_copies(e):
        return [pltpu.make_async_copy(src.at[layer, e], dst, sem_w)
                for src, dst in ((wg_hbm, wgs), (wu_hbm, wus), (wd_hbm, wds))]

    def gather_start(t, p, r):
        pltpu.make_async_copy(_packed_rows(x_hbm, t, 1), _packed_rows(xbufs[p], r, 1), sem_g.at[p]).start()

    def scatter_start(d, p, r):
        pltpu.make_async_copy(_packed_rows(ybufs[p], r, 1), _packed_rows(ys_hbm, d, 1),
                              sem_s.at[p]).start(priority=SCATTER_DMA_QUEUE)

    def gather_wait(p):
        pltpu.make_async_copy(_packed_rows(x_hbm, 0, MOE_BLOCK), _packed_rows(xbufs[p], 0, MOE_BLOCK),
                              sem_g.at[p]).wait()

    def scatter_wait(p):
        pltpu.make_async_copy(_packed_rows(ybufs[p], 0, MOE_BLOCK), _packed_rows(ys_hbm, 0, MOE_BLOCK),
                              sem_s.at[p]).wait()

    @pl.when(jnp.logical_and(j == 0, n_used > 0))
    def _():
        first = idx_copy(0)
        first.start()
        first.wait()

        def issue(r, carry):
            gather_start(idx_smem[0, 0, r], 0, r)
            return carry

        lax.fori_loop(0, MOE_BLOCK, issue, 0, unroll=8)
        idx_copy(1).start()
        spare = pltpu.make_async_copy(idx_hbm.at[n_blocks], idx_smem.at[IDX_SLOTS - 1], sem_i.at[IDX_SLOTS - 1])
        spare.start()
        spare.wait()
        ybuf1[...] = jnp.zeros_like(ybuf1)
        for cp in weight_copies(be_ref[0]):
            cp.start(priority=WEIGHT_DMA_QUEUE)

    @pl.when(j >= n_used)
    def _():
        ybuf0[...] = jnp.zeros_like(ybuf0)
        fill = pltpu.make_async_copy(ybuf0, _packed_rows(ys_hbm, j * MOE_BLOCK, MOE_BLOCK), sem_s.at[0])
        fill.start()
        fill.wait()

    @pl.when(j < n_used)
    def _():
        e = be_ref[j]

        @pl.when(jnp.logical_or(j == 0, e != be_ref[jnp.maximum(j - 1, 0)]))
        def _():
            for cp in weight_copies(e):
                cp.wait()
            wgb[...] = wgs[...].astype(BF16)
            wub[...] = wus[...].astype(BF16)
            wdb[...] = wds[...].astype(BF16)

            @pl.when(ne_ref[j] != e)
            def _():
                for cp in weight_copies(ne_ref[j]):
                    cp.start(priority=WEIGHT_DMA_QUEUE)

        for cur in (0, 1):
            nxt = 1 - cur

            @pl.when(j % 2 == cur)
            def _():
                @pl.when(j >= 1)
                def _():
                    scatter_wait(cur)

                idx_copy(j + 1).wait()
                gather_wait(cur)
                g_slot = (j + 1) % IDX_SLOTS
                s_slot = (j + IDX_SLOTS - 1) % IDX_SLOTS
                for r in range(MOE_BLOCK):
                    gather_start(idx_smem[g_slot, 0, r], nxt, r)
                    scatter_start(idx_smem[s_slot, 1, r], nxt, r)
                pieces = [_load_packed(xbufs[cur], 0, MOE_BLOCK, s) for s in range(ROW_TILES)]
                x = jnp.concatenate([p[0].astype(BF16) for p in pieces] + [p[1].astype(BF16) for p in pieces],
                                    axis=1)
                _store_packed(ybufs[cur], _swiglu(x, wgb[...], wub[...], wdb[...]))
                idx_copy(j + 2).start()

                @pl.when(j == n_used - 1)
                def _():
                    def issue(r, carry):
                        scatter_start(idx_smem[j % IDX_SLOTS, 1, r], cur, r)
                        return carry

                    lax.fori_loop(0, MOE_BLOCK, issue, 0, unroll=8)
                    scatter_wait(nxt)
                    scatter_wait(cur)
                    gather_wait(nxt)
                    idx_copy(j + 2).wait()


def routed_experts(x_packed, idx_rows, block_expert, next_expert, n_used, wg, wu, wd, layer, n_out_rows):
    n_blocks = block_expert.shape[0]
    dh = wg.shape[3]
    any_spec = pl.BlockSpec(memory_space=pl.ANY)
    row_buf = pltpu.VMEM((MOE_BLOCK * ROW_TILES, LANES), jnp.uint32)
    grid_spec = pltpu.PrefetchScalarGridSpec(
        num_scalar_prefetch=3,
        grid=(n_blocks,),
        in_specs=[any_spec] * 5,
        out_specs=any_spec,
        scratch_shapes=[
            pltpu.SMEM((IDX_SLOTS, 2, MOE_BLOCK), jnp.int32),
            row_buf, row_buf, row_buf, row_buf,
            pltpu.VMEM((D, dh), F32), pltpu.VMEM((D, dh), F32), pltpu.VMEM((dh, D), F32),
            pltpu.VMEM((D, dh), BF16), pltpu.VMEM((D, dh), BF16), pltpu.VMEM((dh, D), BF16),
            pltpu.SemaphoreType.DMA((IDX_SLOTS,)), pltpu.SemaphoreType.DMA((2,)), pltpu.SemaphoreType.DMA((2,)),
            pltpu.SemaphoreType.DMA,
        ],
    )
    return pl.pallas_call(
        functools.partial(_routed_kernel, layer=layer),
        grid_spec=grid_spec,
        out_shape=jax.ShapeDtypeStruct((n_out_rows * ROW_TILES, LANES), jnp.uint32),
        compiler_params=_params(("arbitrary",)),
        name="routed_experts",
    )(block_expert, next_expert, n_used, idx_rows, x_packed, wg, wu, wd)


COMBINE_T = 128


def _combine_kernel(ys_ref, wt_ref, f_ref, wg_ref, wu_ref, wd_ref, h_ref, gate_ref, o_ref, wgb, wub, wdb):
    @pl.when(pl.program_id(0) == 0)
    def _():
        wgb[...] = wg_ref[...].astype(BF16)
        wub[...] = wu_ref[...].astype(BF16)
        wdb[...] = wd_ref[...].astype(BF16)

    shared = _swiglu(f_ref[...].astype(BF16), wgb[...], wub[...], wdb[...])
    wts = [wt_ref[:, k:k + 1] for k in range(TOP_K)]
    for s in range(ROW_TILES):
        c_lo = slice(s * LANES, (s + 1) * LANES)
        c_hi = slice(HALF + s * LANES, HALF + (s + 1) * LANES)
        acc_lo = shared[:, c_lo]
        acc_hi = shared[:, c_hi]
        for k in range(TOP_K):
            lo, hi = _load_packed(ys_ref, k * COMBINE_T, COMBINE_T, s)
            acc_lo = acc_lo + wts[k] * lo
            acc_hi = acc_hi + wts[k] * hi
        o_ref[:, c_lo] = h_ref[:, c_lo] + gate_ref[:, c_lo] * acc_lo
        o_ref[:, c_hi] = h_ref[:, c_hi] + gate_ref[:, c_hi] * acc_hi


def combine(ys, wts, f, wsg, wsu, wsd, layer, h, mod3, gate_idx, rows):
    per_tt = TT // COMBINE_T
    dh = wsg.shape[2]
    row = pl.BlockSpec((COMBINE_T, D), lambda i: (i, 0))
    return pl.pallas_call(
        _combine_kernel,
        grid=(rows.n_rows // COMBINE_T,),
        in_specs=[
            pl.BlockSpec((COMBINE_T * TOP_K * ROW_TILES, LANES), lambda i: (i, 0)),
            pl.BlockSpec((COMBINE_T, TOP_K), lambda i: (i, 0)),
            row,
            pl.BlockSpec((None, D, dh), lambda i: (layer, 0, 0)),
            pl.BlockSpec((None, D, dh), lambda i: (layer, 0, 0)),
            pl.BlockSpec((None, dh, D), lambda i: (layer, 0, 0)),
            row,
            pl.BlockSpec((None, 1, D), lambda i: (rows.mod_row(i // per_tt) * N_MOD + gate_idx, 0, 0)),
        ],
        out_specs=row,
        out_shape=jax.ShapeDtypeStruct((rows.n_rows, D), F32),
        scratch_shapes=[pltpu.VMEM((D, dh), BF16), pltpu.VMEM((D, dh), BF16), pltpu.VMEM((dh, D), BF16)],
        compiler_params=_params(("arbitrary",)),
        name="combine",
    )(ys, wts, f, wsg, wsu, wsd, h, mod3)


def moe_layer(h, f, f_packed, mod3, gate_idx, layer, rows, w_router, b_router, wg, wu, wd, wsg, wsu, wsd):
    n = rows.n_rows
    idx, wts, rank, counts = router(f, w_router, b_router)
    counts = counts[:, 0]
    padded = (counts + MOE_BLOCK - 1) // MOE_BLOCK * MOE_BLOCK
    pad_end = jnp.cumsum(padded)
    pad_start = pad_end - padded
    onehot = idx[:, :, None] == jnp.arange(N_EXPERTS, dtype=jnp.int32)[None, None, :]
    dest = jnp.sum(jnp.where(onehot, pad_start[None, None, :], 0), axis=-1) + rank
    n_blocks = -(-(n * TOP_K + N_EXPERTS * (MOE_BLOCK - 1)) // MOE_BLOCK)
    cap = n_blocks * MOE_BLOCK
    n_real = n * TOP_K
    e_ids = jnp.arange(N_EXPERTS, dtype=jnp.int32)
    block_start = jnp.arange(n_blocks, dtype=jnp.int32) * MOE_BLOCK
    block_expert = jnp.sum(block_start[:, None] >= pad_end[None, :], axis=1)
    block_expert = jnp.minimum(block_expert, N_EXPERTS - 1).astype(jnp.int32)
    n_used = (pad_end[-1:] // MOE_BLOCK).astype(jnp.int32)
    real_before = jnp.sum(jnp.where(block_expert[:, None] == e_ids[None, :], jnp.cumsum(counts)[None, :], 0),
                          axis=1)
    pos = jnp.arange(cap, dtype=jnp.int32)
    spare_row = n_real + pos - jnp.repeat(real_before, MOE_BLOCK)
    tile_rows = TOP_K * COMBINE_T
    tok = jnp.arange(n, dtype=jnp.int32)[None, :]
    out_row = tok // COMBINE_T * tile_rows + jnp.arange(TOP_K, dtype=jnp.int32)[:, None] * COMBINE_T + tok % COMBINE_T
    out_rows = spare_row.at[dest.reshape(-1)].set(out_row.reshape(-1))
    src_tok = jnp.where(out_rows >= n_real, pos % n, out_rows // tile_rows * COMBINE_T + out_rows % COMBINE_T)
    idx_rows = jnp.stack([src_tok.reshape(n_blocks, MOE_BLOCK), out_rows.reshape(n_blocks, MOE_BLOCK)], axis=1)
    last_rows = cap - MOE_BLOCK + jnp.arange(MOE_BLOCK, dtype=jnp.int32)
    idx_rows = jnp.concatenate([idx_rows, jnp.stack([last_rows % n, last_rows])[None]], axis=0)
    later = jnp.logical_and(e_ids[None, :] > e_ids[:, None], counts[None, :] > 0)
    next_by_expert = jnp.min(jnp.where(later, e_ids[None, :], N_EXPERTS), axis=1)
    next_by_expert = jnp.where(next_by_expert == N_EXPERTS, e_ids, next_by_expert)
    next_expert = jnp.sum(jnp.where(block_expert[:, None] == e_ids[None, :], next_by_expert[None, :], 0),
                          axis=1).astype(jnp.int32)
    ys = routed_experts(f_packed, idx_rows, block_expert, next_expert, n_used, wg, wu, wd, layer, cap)
    return combine(ys, wts.T, f, wsg, wsu, wsd, layer, h, mod3, gate_idx, rows)


def _rope_tables(rot_dim):
    rows = T // GRID_W
    row = np.repeat(np.arange(rows, dtype=np.float32), GRID_W)
    col = np.tile(np.arange(GRID_W, dtype=np.float32), rows)
    half = rot_dim // 2
    inv_freq = jnp.asarray(ROPE_THETA, F32) ** (-jnp.arange(0, half, 2, dtype=F32) / half)
    ang_r = jnp.asarray(row)[:, None] * inv_freq[None, :]
    ang_c = jnp.asarray(col)[:, None] * inv_freq[None, :]
    ang = jnp.concatenate([ang_r, ang_r, ang_c, ang_c], axis=-1)
    cos, sin = jnp.cos(ang), jnp.sin(ang)
    quarter = (np.arange(rot_dim) // (rot_dim // 4)) % 2
    s1 = jnp.where(quarter[None, :] == 0, -sin, 0.0)
    s2 = jnp.where(quarter[None, :] == 1, sin, 0.0)

    def full(tbl, fill):
        tbl = jnp.pad(tbl, ((0, 0), (0, LANES - rot_dim)), constant_values=fill)
        return jnp.concatenate([jnp.full((CTX, LANES), fill, F32), tbl], axis=0)

    return full(cos, 1.0), full(s1, 0.0), full(s2, 0.0)


def _pad_heads(w, n_heads, width, new_width):
    k = w.shape[0]
    w = w.reshape(k, n_heads, width)
    return jnp.pad(w, ((0, 0), (0, 0), (0, new_width - width))).reshape(k, n_heads * new_width)


def _pad_gain(g, new_width):
    return jnp.pad(g, (0, new_width - g.shape[0])).reshape(1, new_width)


def mla_mixer(a, tables, w_down, g_q, g_kv, w_uq, w_ukv, g_qn, g_kn, w_o_args, with_ctx):
    c, s1, s2 = tables
    w_down_p = jnp.pad(w_down, ((0, 0), (0, LANES - MLA_ROPE)))
    cq, ckv, k_rope = matmul(
        a, w_down_p, tn=w_down_p.shape[1], extra=(g_q.reshape(1, -1), g_kv.reshape(1, -1)),
        extra_specs=(_const_spec(MLA_Q_RANK), _const_spec(MLA_KV_RANK)),
        out_shapes=[jax.ShapeDtypeStruct((N, MLA_Q_RANK), BF16), jax.ShapeDtypeStruct((N, MLA_KV_RANK), BF16),
                    jax.ShapeDtypeStruct((N, LANES), F32)],
        out_specs=[_row_spec(MLA_Q_RANK), _row_spec(MLA_KV_RANK), _row_spec(LANES)],
        epilogue=_epi_mla_down, name="mla_down")

    tn = 1024
    scale = MLA_QK ** -0.5 * LOG2E
    w_uq_p = _pad_heads(w_uq, MLA_HEADS, MLA_QK, MLA_HW)
    rope_specs = (_const_spec(MLA_HW), _table_spec(1), _table_spec(1), _table_spec(1))
    q = matmul(cq, w_uq_p, tn=tn, extra=(_pad_gain(g_qn * scale, MLA_HW), c, s1, s2), extra_specs=rope_specs,
               out_shapes=[jax.ShapeDtypeStruct((N, MLA_HEADS * MLA_HW), BF16)], out_specs=[_row_spec(tn, 1)],
               epilogue=_epi_mla_q, lag=1, name="mla_up_q")[0]
    k, v = matmul(ckv, w_ukv, tn=tn, extra=(k_rope, _pad_gain(g_kn, MLA_HW), c, s1, s2),
                  extra_specs=(pl.BlockSpec((TT, LANES), lambda n, i: (_lagged(i, 1), 0)), *rope_specs),
                  out_shapes=[jax.ShapeDtypeStruct((N, MLA_HEADS * MLA_HW), BF16),
                              jax.ShapeDtypeStruct((N, MLA_HEADS * MLA_V), BF16)],
                  out_specs=[_row_spec(tn, 1), _row_spec(tn // 2, 1)],
                  epilogue=_epi_mla_kv, lag=1, name="mla_up_kv")
    heads_step = 4
    o = attention(q, k, v, n_steps=MLA_HEADS // heads_step, n_heads=heads_step, kv_shared=False,
                  dqk=MLA_HW, dv=MLA_V, with_ctx=with_ctx)
    return _out_proj(o, *w_o_args)


def gqa_mixer(a, tables, w_qkv, g_qn, g_kn, w_o_args, with_ctx):
    c, s1, s2 = tables
    tn = 512
    scale = GQA_HD ** -0.5 * LOG2E
    rope_specs = (_const_spec(GQA_HD), _table_spec(1), _table_spec(1), _table_spec(1))
    nq = GQA_HEADS * GQA_HD
    nk = GQA_KV_HEADS * GQA_HD
    q = matmul(a, w_qkv, tn=tn, n_cols=nq, extra=((g_qn * scale).reshape(1, -1), c, s1, s2),
               extra_specs=rope_specs, out_shapes=[jax.ShapeDtypeStruct((N, nq), BF16)],
               out_specs=[_row_spec(tn, 1)], epilogue=_epi_head_rope, lag=1, name="gqa_q")[0]
    k = matmul(a, w_qkv, tn=tn, col_off=nq // tn, n_cols=nk, extra=(g_kn.reshape(1, -1), c, s1, s2),
               extra_specs=rope_specs, out_shapes=[jax.ShapeDtypeStruct((N, nk), BF16)],
               out_specs=[_row_spec(tn, 1)], epilogue=_epi_head_rope, lag=1, name="gqa_k")[0]
    v = matmul(a, w_qkv, tn=tn, col_off=(nq + nk) // tn, n_cols=nk,
               out_shapes=[jax.ShapeDtypeStruct((N, nk), BF16)], out_specs=[_row_spec(tn)],
               epilogue=_epi_plain, name="gqa_v")[0]
    o = attention(q, k, v, n_steps=GQA_KV_HEADS, n_heads=GQA_REP, kv_shared=True, dqk=GQA_HD, dv=GQA_HD,
                  with_ctx=with_ctx)
    return _out_proj(o, *w_o_args)


def _out_proj(o, w_o, h, mod3, gate_idx, rows):
    tn = 1024
    return matmul(o, w_o, tn=tn, extra=(h, mod3),
                  extra_specs=(pl.BlockSpec((TT, tn), lambda n, i: (rows.src_tile(i), n)),
                               pl.BlockSpec((None, 1, tn), lambda n, i: (rows.mod_row(i) * N_MOD + gate_idx, 0, n))),
                  out_shapes=[jax.ShapeDtypeStruct((rows.n_rows, D), F32)], out_specs=[_row_spec(tn)],
                  epilogue=_epi_resgate, n_rows=rows.n_rows, name="out_proj")[0]


def kernel(x, c, ctx, c_ctx, ada_w, ada_b, norm1_g, norm2_g, mla_w_down, mla_g_q, mla_g_kv, mla_w_uq,
           mla_w_ukv, mla_g_qn, mla_g_kn, mla_w_o, gqa_w_qkv, gqa_g_qn, gqa_g_kn, gqa_w_o, moe_w_router,
           moe_b_router, moe_w_gate, moe_w_up, moe_w_down, moe_ws_gate, moe_ws_up, moe_ws_down):
    depth = ada_w.shape[0]
    h = jnp.concatenate([ctx, x], axis=1).reshape(N, D)
    c_rows = jnp.concatenate([c_ctx[None, :], c, jnp.zeros((16 - 1 - B, D), F32)], axis=0)
    mod = ada_table(c_rows, ada_w, ada_b)
    tables_a = _rope_tables(MLA_ROPE)
    tables_b = _rope_tables(GQA_HD)
    for i in range(depth):
        with_ctx = i < depth - 1
        rows = ALL_ROWS if with_ctx else LATENT_ROWS
        mod3 = mod[i].reshape(16 * N_MOD, 1, D)
        a = norm_mod(h, norm1_g[i], mod3, 0, 1, BF16)
        j = i // 2
        if i % 2 == 0:
            h = mla_mixer(a, tables_a, mla_w_down[j], mla_g_q[j], mla_g_kv[j], mla_w_uq[j], mla_w_ukv[j],
                          mla_g_qn[j], mla_g_kn[j], (mla_w_o[j], h, mod3, 2, rows), with_ctx)
        else:
            h = gqa_mixer(a, tables_b, gqa_w_qkv[j], gqa_g_qn[j], gqa_g_kn[j], (gqa_w_o[j], h, mod3, 2, rows),
                          with_ctx)
        f, f_packed = norm_mod(h, norm2_g[i], mod3, 3, 4, F32, with_packed=True, rows=rows)
        h = moe_layer(h, f, f_packed, mod3, 5, i, rows, moe_w_router[i], moe_b_router[i], moe_w_gate, moe_w_up,
                      moe_w_down, moe_ws_gate, moe_ws_up, moe_ws_down)
    return h.reshape(B, T, D)
```

```python
import functools
from typing import Callable, NamedTuple

import jax
import jax.numpy as jnp
import numpy as np
from jax import lax
from jax.experimental import pallas as pl
from jax.experimental.pallas import tpu as pltpu

F32 = jnp.float32
BF16 = jnp.bfloat16
HIGHEST = lax.Precision.HIGHEST

D = 2048
B = 8
T = 2048
CTX = 256
S = CTX + T
N = B * S
TT = 256
TILES_B = S // TT
GRID_W = 64
ROPE_THETA = 10000.0
EPS = 1e-6
N_MOD = 6

MLA_HEADS = 16
MLA_Q_RANK = 768
MLA_KV_RANK = 512
MLA_NOPE = 128
MLA_ROPE = 64
MLA_QK = MLA_NOPE + MLA_ROPE
MLA_V = 128
MLA_HW = 256

GQA_HEADS = 16
GQA_KV_HEADS = 4
GQA_REP = GQA_HEADS // GQA_KV_HEADS
GQA_HD = 128

N_EXPERTS = 64
N_GROUPS = 8
GROUP_SIZE = N_EXPERTS // N_GROUPS
TOPK_GROUPS = 4
TOP_K = 8
D_EXPERT = 512
ROUTED_SCALE = 2.5
MOE_BLOCK = 256

LANES = 128
LOG2E = 1.4426950408889634
VMEM_LIMIT = 56 * 1024 * 1024


def _params(sem, vmem=VMEM_LIMIT):
    return pltpu.CompilerParams(dimension_semantics=sem, vmem_limit_bytes=vmem)


def _mod_row(i):
    return jnp.where(i % TILES_B == 0, 0, i // TILES_B + 1)


class Rows(NamedTuple):
    n_rows: int
    src_tile: Callable
    mod_row: Callable


LAT_TILES = T // TT
ALL_ROWS = Rows(N, lambda i: i, _mod_row)
LATENT_ROWS = Rows(B * T, lambda i: i // LAT_TILES * TILES_B + 1 + i % LAT_TILES, lambda i: i // LAT_TILES + 1)


def _ada_kernel(c_ref, w_ref, b_ref, o_ref):
    a = c_ref[...]
    a = a * jax.nn.sigmoid(a)
    o_ref[...] = jnp.dot(a, w_ref[...], preferred_element_type=F32, precision=HIGHEST) + b_ref[...]


def ada_table(c_rows, ada_w, ada_b):
    depth = ada_w.shape[0]
    tn = 1024
    return pl.pallas_call(
        _ada_kernel,
        grid=(depth, N_MOD * D // tn),
        in_specs=[
            pl.BlockSpec((16, D), lambda l, n: (0, 0)),
            pl.BlockSpec((None, D, tn), lambda l, n: (l, 0, n)),
            pl.BlockSpec((None, 1, tn), lambda l, n: (l, 0, n)),
        ],
        out_specs=pl.BlockSpec((None, 16, tn), lambda l, n: (l, 0, n)),
        out_shape=jax.ShapeDtypeStruct((depth, 16, N_MOD * D), F32),
        compiler_params=_params(("parallel", "parallel")),
        name="ada_table",
    )(c_rows, ada_w, ada_b.reshape(depth, 1, N_MOD * D))


HALF = D // 2
ROW_TILES = HALF // LANES


def _store_packed(ref, y):
    rows = y.shape[0]
    for s in range(ROW_TILES):
        lo = y[:, s * LANES:(s + 1) * LANES]
        hi = y[:, HALF + s * LANES:HALF + (s + 1) * LANES]
        ref[pl.ds(s, rows, stride=ROW_TILES), :] = pltpu.pack_elementwise([lo, hi], packed_dtype=BF16)


def _load_packed(ref, base, rows, s):
    u = ref[pl.ds(base * ROW_TILES + s, rows, stride=ROW_TILES), :]
    lo = pltpu.unpack_elementwise(u, index=0, packed_dtype=BF16, unpacked_dtype=F32)
    hi = pltpu.unpack_elementwise(u, index=1, packed_dtype=BF16, unpacked_dtype=F32)
    return lo, hi


def _norm_mod_kernel(h_ref, g_ref, sh_ref, sc_ref, o_ref, *packed_ref):
    x = h_ref[...]
    ms = jnp.mean(x * x, axis=-1, keepdims=True)
    y = x * lax.rsqrt(ms + EPS) * g_ref[...]
    y = y * (1.0 + sc_ref[...]) + sh_ref[...]
    o_ref[...] = y.astype(o_ref.dtype)
    if packed_ref:
        _store_packed(packed_ref[0], y)


def norm_mod(h, gain, mod3, shift_idx, scale_idx, out_dtype, with_packed=False, rows=ALL_ROWS):
    n = rows.n_rows
    row = pl.BlockSpec((TT, D), lambda i: (i, 0))
    out_specs, out_shape = [row], [jax.ShapeDtypeStruct((n, D), out_dtype)]
    if with_packed:
        out_specs.append(pl.BlockSpec((TT * ROW_TILES, LANES), lambda i: (i, 0)))
        out_shape.append(jax.ShapeDtypeStruct((n * ROW_TILES, LANES), jnp.uint32))
    outs = pl.pallas_call(
        _norm_mod_kernel,
        grid=(n // TT,),
        in_specs=[
            row,
            pl.BlockSpec((1, D), lambda i: (0, 0)),
            pl.BlockSpec((None, 1, D), lambda i: (rows.mod_row(i) * N_MOD + shift_idx, 0, 0)),
            pl.BlockSpec((None, 1, D), lambda i: (rows.mod_row(i) * N_MOD + scale_idx, 0, 0)),
        ],
        out_specs=out_specs,
        out_shape=out_shape,
        compiler_params=_params(("parallel",)),
        name="norm_mod",
    )(h, gain.reshape(1, D), mod3, mod3)
    return outs if with_packed else outs[0]


def _mm_kernel(*refs, n_extra, n_out, epilogue, lag):
    a_ref, w_ref = refs[0], refs[1]
    extra = refs[2:2 + n_extra]
    outs = refs[2 + n_extra:2 + n_extra + n_out]
    wb_ref = refs[2 + n_extra + n_out]
    i = pl.program_id(1)

    @pl.when(i == 0)
    def _():
        wb_ref[...] = w_ref[...].astype(BF16)

    def product():
        return jnp.dot(a_ref[...].astype(BF16), wb_ref[...], preferred_element_type=F32)

    if not lag:
        epilogue(product(), extra, outs)
        return

    acc0, acc1 = refs[3 + n_extra + n_out:]

    @pl.when(i == 0)
    def _():
        acc1[...] = jnp.zeros_like(acc1)

    for parity, (cur, prev) in enumerate(((acc0, acc1), (acc1, acc0))):
        @pl.when(i % 2 == parity)
        def _():
            epilogue(prev[...], extra, outs)
            cur[...] = product()


def _lagged(i, lag):
    return jnp.maximum(i - lag, 0)


def matmul(a, w, *, tn, col_off=0, n_cols=None, extra=(), extra_specs=(), out_shapes, out_specs, epilogue,
           name, lag=0, n_rows=N):
    k = a.shape[1]
    n_cols = w.shape[1] if n_cols is None else n_cols
    n_tiles = n_rows // TT
    kern = functools.partial(_mm_kernel, n_extra=len(extra), n_out=len(out_shapes), epilogue=epilogue, lag=lag)
    acc = [pltpu.VMEM((TT, tn), F32)] * 2 if lag else []
    return pl.pallas_call(
        kern,
        grid=(n_cols // tn, n_tiles + lag),
        in_specs=[
            pl.BlockSpec((TT, k), lambda n, i: (jnp.minimum(i, n_tiles - 1), 0)),
            pl.BlockSpec((k, tn), lambda n, i: (0, n + col_off)),
            *extra_specs,
        ],
        out_specs=out_specs,
        out_shape=out_shapes,
        scratch_shapes=[pltpu.VMEM((k, tn), BF16), *acc],
        compiler_params=_params(("parallel", "arbitrary")),
        name=name,
    )(a, w, *extra)


def _epi_plain(acc, extra, outs):
    outs[0][...] = acc.astype(outs[0].dtype)


def _rms(x, g_ref):
    return x * lax.rsqrt(jnp.mean(x * x, axis=-1, keepdims=True) + EPS) * g_ref[...]


def _epi_mla_down(acc, extra, outs):
    gq_ref, gkv_ref = extra
    cq_out, ckv_out, rope_out = outs
    kv_end = MLA_Q_RANK + MLA_KV_RANK
    cq_out[...] = _rms(acc[:, :MLA_Q_RANK], gq_ref).astype(cq_out.dtype)
    ckv_out[...] = _rms(acc[:, MLA_Q_RANK:kv_end], gkv_ref).astype(ckv_out.dtype)
    rope_out[...] = acc[:, kv_end:]


def _epi_resgate(acc, extra, outs):
    res_ref, gate_ref = extra
    outs[0][...] = res_ref[...] + gate_ref[...] * acc


def _rope(x, c, s1, s2, quarter):
    return x * c + pltpu.roll(x, LANES - quarter, 1) * s1 + pltpu.roll(x, quarter, 1) * s2


def _epi_head_rope(acc, extra, outs):
    g_ref, c_ref, s1_ref, s2_ref = extra
    c, s1, s2 = c_ref[...], s1_ref[...], s2_ref[...]
    for j in range(acc.shape[1] // LANES):
        x = acc[:, j * LANES:(j + 1) * LANES]
        ms = jnp.mean(x * x, axis=-1, keepdims=True)
        xn = x * lax.rsqrt(ms + EPS) * g_ref[...]
        outs[0][:, j * LANES:(j + 1) * LANES] = _rope(xn, c, s1, s2, GQA_HD // 4).astype(outs[0].dtype)


def _mla_head(nope, rope, g_ref, c, s1, s2):
    ss = jnp.sum(nope * nope, axis=-1, keepdims=True) + jnp.sum(rope * rope, axis=-1, keepdims=True)
    r = lax.rsqrt(ss * (1.0 / MLA_QK) + EPS)
    nope_n = nope * r * g_ref[:, :LANES]
    rope_n = _rope(rope * r * g_ref[:, LANES:], c, s1, s2, MLA_ROPE // 4)
    return nope_n, rope_n


def _epi_mla_q(acc, extra, outs):
    g_ref, c_ref, s1_ref, s2_ref = extra
    c, s1, s2 = c_ref[...], s1_ref[...], s2_ref[...]
    for j in range(acc.shape[1] // MLA_HW):
        nope = acc[:, j * MLA_HW:j * MLA_HW + LANES]
        rope = acc[:, j * MLA_HW + LANES:(j + 1) * MLA_HW]
        nope_n, rope_n = _mla_head(nope, rope, g_ref, c, s1, s2)
        dst = j * MLA_HW
        outs[0][:, dst:dst + LANES] = nope_n.astype(outs[0].dtype)
        outs[0][:, dst + LANES:dst + MLA_HW] = rope_n.astype(outs[0].dtype)


def _epi_mla_kv(acc, extra, outs):
    kr_ref, g_ref, c_ref, s1_ref, s2_ref = extra
    k_out, v_out = outs
    c, s1, s2 = c_ref[...], s1_ref[...], s2_ref[...]
    rope = kr_ref[...]
    for j in range(acc.shape[1] // MLA_HW):
        nope = acc[:, j * MLA_HW:j * MLA_HW + LANES]
        v = acc[:, j * MLA_HW + LANES:(j + 1) * MLA_HW]
        nope_n, rope_n = _mla_head(nope, rope, g_ref, c, s1, s2)
        dst = j * MLA_HW
        k_out[:, dst:dst + LANES] = nope_n.astype(k_out.dtype)
        k_out[:, dst + LANES:dst + MLA_HW] = rope_n.astype(k_out.dtype)
        v_out[:, dst // 2:dst // 2 + LANES] = v.astype(v_out.dtype)


def _row_spec(width, lag=0):
    return pl.BlockSpec((TT, width), lambda n, i: (_lagged(i, lag), n))


def _const_spec(width):
    return pl.BlockSpec((1, width), lambda n, i: (0, 0))


def _table_spec(lag=0):
    return pl.BlockSpec((TT, LANES), lambda n, i: (_lagged(i, lag) % TILES_B, 0))


KEY_CHUNK = 768


def _attn_kernel(*refs, n_heads, kv_shared, dqk, dv, with_ctx):
    if with_ctx:
        qc_ref, ql_ref, k_ref, v_ref, oc_ref, ol_ref = refs
    else:
        ql_ref, k_ref, v_ref, ol_ref = refs

    def compute(q_ref, o_ref, nk):
        chunk = min(KEY_CHUNK, nk)
        n_chunks = nk // chunk

        def scores(h, c):
            kh = 0 if kv_shared else h
            q = q_ref[:, h * dqk:(h + 1) * dqk]
            k = k_ref[c * chunk:(c + 1) * chunk, kh * dqk:(kh + 1) * dqk]
            return lax.dot_general(q, k, (((1,), (1,)), ((), ())), preferred_element_type=F32)

        def row_max(s_chunks):
            m = jnp.max(s_chunks[0], axis=-1, keepdims=True)
            for s in s_chunks[1:]:
                m = jnp.maximum(m, jnp.max(s, axis=-1, keepdims=True))
            return m

        s_cur = [scores(0, c) for c in range(n_chunks)]
        for h in range(n_heads):
            vh = 0 if kv_shared else h
            m = row_max(s_cur)
            s_next = []
            l = None
            o = None
            for c in range(n_chunks):
                p = jnp.exp2(s_cur[c] - m)
                l_c = jnp.sum(p, axis=-1, keepdims=True)
                v = v_ref[c * chunk:(c + 1) * chunk, vh * dv:(vh + 1) * dv]
                o_c = jnp.dot(p.astype(BF16), v, preferred_element_type=F32)
                l = l_c if l is None else l + l_c
                o = o_c if o is None else o + o_c
                if h + 1 < n_heads:
                    s_next.append(scores(h + 1, c))
            o_ref[:, h * dv:(h + 1) * dv] = (o / l).astype(o_ref.dtype)
            s_cur = s_next

    if with_ctx:
        @pl.when(pl.program_id(2) == 0)
        def _():
            compute(qc_ref, oc_ref, CTX)

        @pl.when(pl.program_id(2) > 0)
        def _():
            compute(ql_ref, ol_ref, S)
    else:
        compute(ql_ref, ol_ref, S)


LAT_Q_ROWS = 2 * TT


def attention(q, k, v, *, n_steps, n_heads, kv_shared, dqk, dv, with_ctx):
    kv_heads = 1 if kv_shared else n_heads
    q_cols, o_cols = n_heads * dqk, n_heads * dv
    lat_blocks = T // LAT_Q_ROWS
    first = 1 if with_ctx else 0

    def lat_block(i):
        return jnp.maximum(i - first, 0)

    ql_spec = pl.BlockSpec((pl.Element(LAT_Q_ROWS), pl.Element(q_cols)),
                           lambda b, g, i: (pl.multiple_of(b * S + CTX + lat_block(i) * LAT_Q_ROWS, TT),
                                            pl.multiple_of(g * q_cols, LANES)))
    k_spec = pl.BlockSpec((S, kv_heads * dqk), lambda b, g, i: (b, g))
    v_spec = pl.BlockSpec((S, kv_heads * dv), lambda b, g, i: (b, g))
    ol_spec = pl.BlockSpec((LAT_Q_ROWS, o_cols), lambda b, g, i: (b * lat_blocks + lat_block(i), g))
    ol_shape = jax.ShapeDtypeStruct((B * T, n_steps * o_cols), BF16)
    if with_ctx:
        in_specs = [pl.BlockSpec((CTX, q_cols), lambda b, g, i: (b * TILES_B, g)), ql_spec, k_spec, v_spec]
        out_specs = [pl.BlockSpec((CTX, o_cols), lambda b, g, i: (b, g)), ol_spec]
        out_shape = [jax.ShapeDtypeStruct((B * CTX, n_steps * o_cols), BF16), ol_shape]
        args = (q, q, k, v)
    else:
        in_specs, out_specs, out_shape, args = [ql_spec, k_spec, v_spec], [ol_spec], [ol_shape], (q, k, v)
    kern = functools.partial(_attn_kernel, n_heads=n_heads, kv_shared=kv_shared, dqk=dqk, dv=dv,
                             with_ctx=with_ctx)
    outs = pl.pallas_call(
        kern,
        grid=(B, n_steps, first + lat_blocks),
        in_specs=in_specs,
        out_specs=out_specs,
        out_shape=out_shape,
        compiler_params=_params(("parallel", "parallel", "arbitrary")),
        name="attention",
    )(*args)
    return outs if with_ctx else (None, outs[0])


def _first_index(hit, iota, size):
    return jnp.min(jnp.where(hit, iota, float(size)), axis=0, keepdims=True)


def _router_kernel(f_ref, wr_ref, br_ref, idx_ref, wt_ref, rank_ref, cnt_ref, carry_ref):
    @pl.when(pl.program_id(0) == 0)
    def _():
        carry_ref[...] = jnp.zeros_like(carry_ref)

    neg = -jnp.inf
    logits = lax.dot_general(wr_ref[...], f_ref[...], (((1,), (1,)), ((), ())),
                             preferred_element_type=F32, precision=HIGHEST)
    scores = jax.nn.sigmoid(logits)
    biased = scores + br_ref[...]

    iota_m = lax.broadcasted_iota(jnp.int32, (GROUP_SIZE, TT), 0).astype(F32)
    groups = [biased[g * GROUP_SIZE:(g + 1) * GROUP_SIZE, :] for g in range(N_GROUPS)]
    gs_rows = []
    for blk in groups:
        m1 = jnp.max(blk, axis=0, keepdims=True)
        i1 = _first_index(blk == m1, iota_m, GROUP_SIZE)
        m2 = jnp.max(jnp.where(iota_m == i1, neg, blk), axis=0, keepdims=True)
        gs_rows.append(m1 + m2)
    gs = jnp.concatenate(gs_rows, axis=0)

    iota_g = lax.broadcasted_iota(jnp.int32, gs.shape, 0).astype(F32)
    sel = jnp.zeros(gs.shape, F32)
    cur = gs
    for _ in range(TOPK_GROUPS):
        m = jnp.max(cur, axis=0, keepdims=True)
        hit = iota_g == _first_index(cur == m, iota_g, N_GROUPS)
        sel = jnp.where(hit, 1.0, sel)
        cur = jnp.where(hit, neg, cur)

    cur = jnp.concatenate(
        [jnp.where(sel[g:g + 1, :] > 0.5, groups[g], neg) for g in range(N_GROUPS)], axis=0)
    iota_e = lax.broadcasted_iota(jnp.int32, cur.shape, 0).astype(F32)
    assigned = jnp.zeros(cur.shape, F32)
    w_rows, hits = [], []
    for k in range(TOP_K):
        m = jnp.max(cur, axis=0, keepdims=True)
        first = _first_index(cur == m, iota_e, N_EXPERTS)
        hit = iota_e == first
        idx_ref[k:k + 1, :] = first.astype(jnp.int32)
        w_rows.append(jnp.sum(jnp.where(hit, scores, 0.0), axis=0, keepdims=True))
        hits.append(hit)
        assigned = jnp.where(hit, 1.0, assigned)
        cur = jnp.where(hit, neg, cur)

    w_sum = w_rows[0]
    for k in range(1, TOP_K):
        w_sum = w_sum + w_rows[k]
    for k in range(TOP_K):
        wt_ref[k:k + 1, :] = w_rows[k] / w_sum * ROUTED_SCALE

    r_i = lax.broadcasted_iota(jnp.int32, (TT, TT), 0)
    c_i = lax.broadcasted_iota(jnp.int32, (TT, TT), 1)
    upper = jnp.where(r_i <= c_i, 1.0, 0.0).astype(BF16)
    incl = jnp.dot(assigned.astype(BF16), upper, preferred_element_type=F32)
    rank_e = carry_ref[...] + incl - assigned
    for k in range(TOP_K):
        rank_k = jnp.sum(jnp.where(hits[k], rank_e, 0.0), axis=0, keepdims=True)
        rank_ref[k:k + 1, :] = rank_k.astype(jnp.int32)
    carry = carry_ref[...] + jnp.sum(assigned, axis=1, keepdims=True)
    carry_ref[...] = carry
    cnt_ref[...] = carry.astype(jnp.int32)


def router(f, w_router, b_router):
    n = f.shape[0]
    tok = pl.BlockSpec((TOP_K, TT), lambda i: (0, i))
    return pl.pallas_call(
        _router_kernel,
        grid=(n // TT,),
        in_specs=[
            pl.BlockSpec((TT, D), lambda i: (i, 0)),
            pl.BlockSpec((N_EXPERTS, D), lambda i: (0, 0)),
            pl.BlockSpec((N_EXPERTS, 1), lambda i: (0, 0)),
        ],
        out_specs=[tok, tok, tok, pl.BlockSpec((N_EXPERTS, 1), lambda i: (0, 0))],
        out_shape=[
            jax.ShapeDtypeStruct((TOP_K, n), jnp.int32),
            jax.ShapeDtypeStruct((TOP_K, n), F32),
            jax.ShapeDtypeStruct((TOP_K, n), jnp.int32),
            jax.ShapeDtypeStruct((N_EXPERTS, 1), jnp.int32),
        ],
        scratch_shapes=[pltpu.VMEM((N_EXPERTS, 1), F32)],
        compiler_params=_params(("arbitrary",)),
        name="router",
    )(f, w_router.T, b_router.reshape(N_EXPERTS, 1))


def _swiglu(x, wg, wu, wd):
    g = jnp.dot(x, wg, preferred_element_type=F32)
    u = jnp.dot(x, wu, preferred_element_type=F32)
    mid = (g * jax.nn.sigmoid(g) * u).astype(BF16)
    return jnp.dot(mid, wd, preferred_element_type=F32)


def _packed_rows(ref, first_row, n_rows):
    start = first_row * ROW_TILES
    if not isinstance(start, int):
        start = pl.multiple_of(start, ROW_TILES)
    return ref.at[pl.ds(start, n_rows * ROW_TILES)]


IDX_SLOTS = 4
WEIGHT_DMA_QUEUE = 1
SCATTER_DMA_QUEUE = 1


def _routed_kernel(be_ref, ne_ref, nu_ref, idx_hbm, x_hbm, wg_hbm, wu_hbm, wd_hbm, ys_hbm,
                   idx_smem, xbuf0, xbuf1, ybuf0, ybuf1, wgs, wus, wds, wgb, wub, wdb,
                   sem_i, sem_g, sem_s, sem_w, *, layer):
    j = pl.program_id(0)
    n_used = nu_ref[0]
    n_blocks = pl.num_programs(0)
    xbufs = (xbuf0, xbuf1)
    ybufs = (ybuf0, ybuf1)

    def idx_copy(blk):
        s = blk % IDX_SLOTS
        return pltpu.make_async_copy(idx_hbm.at[jnp.minimum(blk, n_blocks - 1)], idx_smem.at[s], sem_i.at[s])

    def weight_copies(e):
        return [pltpu.make_async_copy(src.at[layer, e], dst, sem_w)
                for src, dst in ((wg_hbm, wgs), (wu_hbm, wus), (wd_hbm, wds))]

    def gather_start(t, p, r):
        pltpu.make_async_copy(_packed_rows(x_hbm, t, 1), _packed_rows(xbufs[p], r, 1), sem_g.at[p]).start()

    def scatter_start(d, p, r):
        pltpu.make_async_copy(_packed_rows(ybufs[p], r, 1), _packed_rows(ys_hbm, d, 1),
                              sem_s.at[p]).start(priority=SCATTER_DMA_QUEUE)

    def gather_wait(p):
        pltpu.make_async_copy(_packed_rows(x_hbm, 0, MOE_BLOCK), _packed_rows(xbufs[p], 0, MOE_BLOCK),
                              sem_g.at[p]).wait()

    def scatter_wait(p):
        pltpu.make_async_copy(_packed_rows(ybufs[p], 0, MOE_BLOCK), _packed_rows(ys_hbm, 0, MOE_BLOCK),
                              sem_s.at[p]).wait()

    @pl.when(jnp.logical_and(j == 0, n_used > 0))
    def _():
        first = idx_copy(0)
        first.start()
        first.wait()

        def issue(r, carry):
            gather_start(idx_smem[0, 0, r], 0, r)
            return carry

        lax.fori_loop(0, MOE_BLOCK, issue, 0, unroll=8)
        idx_copy(1).start()
        spare = pltpu.make_async_copy(idx_hbm.at[n_blocks], idx_smem.at[IDX_SLOTS - 1], sem_i.at[IDX_SLOTS - 1])
        spare.start()
        spare.wait()
        ybuf1[...] = jnp.zeros_like(ybuf1)
        for cp in weight_copies(be_ref[0]):
            cp.start(priority=WEIGHT_DMA_QUEUE)

    @pl.when(j >= n_used)
    def _():
        ybuf0[...] = jnp.zeros_like(ybuf0)
        fill = pltpu.make_async_copy(ybuf0, _packed_rows(ys_hbm, j * MOE_BLOCK, MOE_BLOCK), sem_s.at[0])
        fill.start()
        fill.wait()

    @pl.when(j < n_used)
    def _():
        e = be_ref[j]

        @pl.when(jnp.logical_or(j == 0, e != be_ref[jnp.maximum(j - 1, 0)]))
        def _():
            for cp in weight_copies(e):
                cp.wait()
            wgb[...] = wgs[...].astype(BF16)
            wub[...] = wus[...].astype(BF16)
            wdb[...] = wds[...].astype(BF16)

            @pl.when(ne_ref[j] != e)
            def _():
                for cp in weight_copies(ne_ref[j]):
                    cp.start(priority=WEIGHT_DMA_QUEUE)

        for cur in (0, 1):
            nxt = 1 - cur

            @pl.when(j % 2 == cur)
            def _():
                @pl.when(j >= 1)
                def _():
                    scatter_wait(cur)

                idx_copy(j + 1).wait()
                gather_wait(cur)
                g_slot = (j + 1) % IDX_SLOTS
                s_slot = (j + IDX_SLOTS - 1) % IDX_SLOTS
                for r in range(MOE_BLOCK):
                    gather_start(idx_smem[g_slot, 0, r], nxt, r)
                    scatter_start(idx_smem[s_slot, 1, r], nxt, r)
                pieces = [_load_packed(xbufs[cur], 0, MOE_BLOCK, s) for s in range(ROW_TILES)]
                x = jnp.concatenate([p[0].astype(BF16) for p in pieces] + [p[1].astype(BF16) for p in pieces],
                                    axis=1)
                _store_packed(ybufs[cur], _swiglu(x, wgb[...], wub[...], wdb[...]))
                idx_copy(j + 2).start()

                @pl.when(j == n_used - 1)
                def _():
                    def issue(r, carry):
                        scatter_start(idx_smem[j % IDX_SLOTS, 1, r], cur, r)
                        return carry

                    lax.fori_loop(0, MOE_BLOCK, issue, 0, unroll=8)
                    scatter_wait(nxt)
                    scatter_wait(cur)
                    gather_wait(nxt)
                    idx_copy(j + 2).wait()


def routed_experts(x_packed, idx_rows, block_expert, next_expert, n_used, wg, wu, wd, layer, n_out_rows):
    n_blocks = block_expert.shape[0]
    dh = wg.shape[3]
    any_spec = pl.BlockSpec(memory_space=pl.ANY)
    row_buf = pltpu.VMEM((MOE_BLOCK * ROW_TILES, LANES), jnp.uint32)
    grid_spec = pltpu.PrefetchScalarGridSpec(
        num_scalar_prefetch=3,
        grid=(n_blocks,),
        in_specs=[any_spec] * 5,
        out_specs=any_spec,
        scratch_shapes=[
            pltpu.SMEM((IDX_SLOTS, 2, MOE_BLOCK), jnp.int32),
            row_buf, row_buf, row_buf, row_buf,
            pltpu.VMEM((D, dh), F32), pltpu.VMEM((D, dh), F32), pltpu.VMEM((dh, D), F32),
            pltpu.VMEM((D, dh), BF16), pltpu.VMEM((D, dh), BF16), pltpu.VMEM((dh, D), BF16),
            pltpu.SemaphoreType.DMA((IDX_SLOTS,)), pltpu.SemaphoreType.DMA((2,)), pltpu.SemaphoreType.DMA((2,)),
            pltpu.SemaphoreType.DMA,
        ],
    )
    return pl.pallas_call(
        functools.partial(_routed_kernel, layer=layer),
        grid_spec=grid_spec,
        out_shape=jax.ShapeDtypeStruct((n_out_rows * ROW_TILES, LANES), jnp.uint32),
        compiler_params=_params(("arbitrary",)),
        name="routed_experts",
    )(block_expert, next_expert, n_used, idx_rows, x_packed, wg, wu, wd)


COMBINE_T = 128


def _combine_kernel(ys_ref, wt_ref, f_ref, wg_ref, wu_ref, wd_ref, h_ref, gate_ref, o_ref, wgb, wub, wdb):
    @pl.when(pl.program_id(0) == 0)
    def _():
        wgb[...] = wg_ref[...].astype(BF16)
        wub[...] = wu_ref[...].astype(BF16)
        wdb[...] = wd_ref[...].astype(BF16)

    shared = _swiglu(f_ref[...].astype(BF16), wgb[...], wub[...], wdb[...])
    wts = [wt_ref[:, k:k + 1] for k in range(TOP_K)]
    for s in range(ROW_TILES):
        c_lo = slice(s * LANES, (s + 1) * LANES)
        c_hi = slice(HALF + s * LANES, HALF + (s + 1) * LANES)
        acc_lo = shared[:, c_lo]
        acc_hi = shared[:, c_hi]
        for k in range(TOP_K):
            lo, hi = _load_packed(ys_ref, k * COMBINE_T, COMBINE_T, s)
            acc_lo = acc_lo + wts[k] * lo
            acc_hi = acc_hi + wts[k] * hi
        o_ref[:, c_lo] = h_ref[:, c_lo] + gate_ref[:, c_lo] * acc_lo
        o_ref[:, c_hi] = h_ref[:, c_hi] + gate_ref[:, c_hi] * acc_hi


def combine(ys, wts, f, wsg, wsu, wsd, layer, h, mod3, gate_idx, rows):
    per_tt = TT // COMBINE_T
    dh = wsg.shape[2]
    row = pl.BlockSpec((COMBINE_T, D), lambda i: (i, 0))
    return pl.pallas_call(
        _combine_kernel,
        grid=(rows.n_rows // COMBINE_T,),
        in_specs=[
            pl.BlockSpec((COMBINE_T * TOP_K * ROW_TILES, LANES), lambda i: (i, 0)),
            pl.BlockSpec((COMBINE_T, TOP_K), lambda i: (i, 0)),
            row,
            pl.BlockSpec((None, D, dh), lambda i: (layer, 0, 0)),
            pl.BlockSpec((None, D, dh), lambda i: (layer, 0, 0)),
            pl.BlockSpec((None, dh, D), lambda i: (layer, 0, 0)),
            row,
            pl.BlockSpec((None, 1, D), lambda i: (rows.mod_row(i // per_tt) * N_MOD + gate_idx, 0, 0)),
        ],
        out_specs=row,
        out_shape=jax.ShapeDtypeStruct((rows.n_rows, D), F32),
        scratch_shapes=[pltpu.VMEM((D, dh), BF16), pltpu.VMEM((D, dh), BF16), pltpu.VMEM((dh, D), BF16)],
        compiler_params=_params(("arbitrary",)),
        name="combine",
    )(ys, wts, f, wsg, wsu, wsd, h, mod3)


def moe_layer(h, f, f_packed, mod3, gate_idx, layer, rows, w_router, b_router, wg, wu, wd, wsg, wsu, wsd):
    n = rows.n_rows
    idx, wts, rank, counts = router(f, w_router, b_router)
    counts = counts[:, 0]
    padded = (counts + MOE_BLOCK - 1) // MOE_BLOCK * MOE_BLOCK
    pad_end = jnp.cumsum(padded)
    pad_start = pad_end - padded
    onehot = idx[:, :, None] == jnp.arange(N_EXPERTS, dtype=jnp.int32)[None, None, :]
    dest = jnp.sum(jnp.where(onehot, pad_start[None, None, :], 0), axis=-1) + rank
    n_blocks = -(-(n * TOP_K + N_EXPERTS * (MOE_BLOCK - 1)) // MOE_BLOCK)
    cap = n_blocks * MOE_BLOCK
    n_real = n * TOP_K
    e_ids = jnp.arange(N_EXPERTS, dtype=jnp.int32)
    block_start = jnp.arange(n_blocks, dtype=jnp.int32) * MOE_BLOCK
    block_expert = jnp.sum(block_start[:, None] >= pad_end[None, :], axis=1)
    block_expert = jnp.minimum(block_expert, N_EXPERTS - 1).astype(jnp.int32)
    n_used = (pad_end[-1:] // MOE_BLOCK).astype(jnp.int32)
    real_before = jnp.sum(jnp.where(block_expert[:, None] == e_ids[None, :], jnp.cumsum(counts)[None, :], 0),
                          axis=1)
    pos = jnp.arange(cap, dtype=jnp.int32)
    spare_row = n_real + pos - jnp.repeat(real_before, MOE_BLOCK)
    tile_rows = TOP_K * COMBINE_T
    tok = jnp.arange(n, dtype=jnp.int32)[None, :]
    out_row = tok // COMBINE_T * tile_rows + jnp.arange(TOP_K, dtype=jnp.int32)[:, None] * COMBINE_T + tok % COMBINE_T
    out_rows = spare_row.at[dest.reshape(-1)].set(out_row.reshape(-1))
    src_tok = jnp.where(out_rows >= n_real, pos % n, out_rows // tile_rows * COMBINE_T + out_rows % COMBINE_T)
    idx_rows = jnp.stack([src_tok.reshape(n_blocks, MOE_BLOCK), out_rows.reshape(n_blocks, MOE_BLOCK)], axis=1)
    last_rows = cap - MOE_BLOCK + jnp.arange(MOE_BLOCK, dtype=jnp.int32)
    idx_rows = jnp.concatenate([idx_rows, jnp.stack([last_rows % n, last_rows])[None]], axis=0)
    later = jnp.logical_and(e_ids[None, :] > e_ids[:, None], counts[None, :] > 0)
    next_by_expert = jnp.min(jnp.where(later, e_ids[None, :], N_EXPERTS), axis=1)
    next_by_expert = jnp.where(next_by_expert == N_EXPERTS, e_ids, next_by_expert)
    next_expert = jnp.sum(jnp.where(block_expert[:, None] == e_ids[None, :], next_by_expert[None, :], 0),
                          axis=1).astype(jnp.int32)
    ys = routed_experts(f_packed, idx_rows, block_expert, next_expert, n_used, wg, wu, wd, layer, cap)
    return combine(ys, wts.T, f, wsg, wsu, wsd, layer, h, mod3, gate_idx, rows)


def _rope_tables(rot_dim):
    rows = T // GRID_W
    row = np.repeat(np.arange(rows, dtype=np.float32), GRID_W)
    col = np.tile(np.arange(GRID_W, dtype=np.float32), rows)
    half = rot_dim // 2
    inv_freq = jnp.asarray(ROPE_THETA, F32) ** (-jnp.arange(0, half, 2, dtype=F32) / half)
    ang_r = jnp.asarray(row)[:, None] * inv_freq[None, :]
    ang_c = jnp.asarray(col)[:, None] * inv_freq[None, :]
    ang = jnp.concatenate([ang_r, ang_r, ang_c, ang_c], axis=-1)
    cos, sin = jnp.cos(ang), jnp.sin(ang)
    quarter = (np.arange(rot_dim) // (rot_dim // 4)) % 2
    s1 = jnp.where(quarter[None, :] == 0, -sin, 0.0)
    s2 = jnp.where(quarter[None, :] == 1, sin, 0.0)

    def full(tbl, fill):
        tbl = jnp.pad(tbl, ((0, 0), (0, LANES - rot_dim)), constant_values=fill)
        return jnp.concatenate([jnp.full((CTX, LANES), fill, F32), tbl], axis=0)

    return full(cos, 1.0), full(s1, 0.0), full(s2, 0.0)


def _pad_heads(w, n_heads, width, new_width):
    k = w.shape[0]
    w = w.reshape(k, n_heads, width)
    return jnp.pad(w, ((0, 0), (0, 0), (0, new_width - width))).reshape(k, n_heads * new_width)


def _pad_gain(g, new_width):
    return jnp.pad(g, (0, new_width - g.shape[0])).reshape(1, new_width)


def mla_mixer(a, tables, w_down, g_q, g_kv, w_uq, w_ukv, g_qn, g_kn, w_o_args, with_ctx):
    c, s1, s2 = tables
    w_down_p = jnp.pad(w_down, ((0, 0), (0, LANES - MLA_ROPE)))
    cq, ckv, k_rope = matmul(
        a, w_down_p, tn=w_down_p.shape[1], extra=(g_q.reshape(1, -1), g_kv.reshape(1, -1)),
        extra_specs=(_const_spec(MLA_Q_RANK), _const_spec(MLA_KV_RANK)),
        out_shapes=[jax.ShapeDtypeStruct((N, MLA_Q_RANK), BF16), jax.ShapeDtypeStruct((N, MLA_KV_RANK), BF16),
                    jax.ShapeDtypeStruct((N, LANES), F32)],
        out_specs=[_row_spec(MLA_Q_RANK), _row_spec(MLA_KV_RANK), _row_spec(LANES)],
        epilogue=_epi_mla_down, name="mla_down")

    tn = 1024
    scale = MLA_QK ** -0.5 * LOG2E
    w_uq_p = _pad_heads(w_uq, MLA_HEADS, MLA_QK, MLA_HW)
    rope_specs = (_const_spec(MLA_HW), _table_spec(1), _table_spec(1), _table_spec(1))
    q = matmul(cq, w_uq_p, tn=tn, extra=(_pad_gain(g_qn * scale, MLA_HW), c, s1, s2), extra_specs=rope_specs,
               out_shapes=[jax.ShapeDtypeStruct((N, MLA_HEADS * MLA_HW), BF16)], out_specs=[_row_spec(tn, 1)],
               epilogue=_epi_mla_q, lag=1, name="mla_up_q")[0]
    k, v = matmul(ckv, w_ukv, tn=tn, extra=(k_rope, _pad_gain(g_kn, MLA_HW), c, s1, s2),
                  extra_specs=(pl.BlockSpec((TT, LANES), lambda n, i: (_lagged(i, 1), 0)), *rope_specs),
                  out_shapes=[jax.ShapeDtypeStruct((N, MLA_HEADS * MLA_HW), BF16),
                              jax.ShapeDtypeStruct((N, MLA_HEADS * MLA_V), BF16)],
                  out_specs=[_row_spec(tn, 1), _row_spec(tn // 2, 1)],
                  epilogue=_epi_mla_kv, lag=1, name="mla_up_kv")
    heads_step = 4
    o = attention(q, k, v, n_steps=MLA_HEADS // heads_step, n_heads=heads_step, kv_shared=False,
                  dqk=MLA_HW, dv=MLA_V, with_ctx=with_ctx)
    return _out_proj(_attn_rows(*o), *w_o_args)


def gqa_mixer(a, tables, w_qkv, g_qn, g_kn, w_o_args, with_ctx):
    c, s1, s2 = tables
    tn = 512
    scale = GQA_HD ** -0.5 * LOG2E
    rope_specs = (_const_spec(GQA_HD), _table_spec(1), _table_spec(1), _table_spec(1))
    nq = GQA_HEADS * GQA_HD
    nk = GQA_KV_HEADS * GQA_HD
    q = matmul(a, w_qkv, tn=tn, n_cols=nq, extra=((g_qn * scale).reshape(1, -1), c, s1, s2),
               extra_specs=rope_specs, out_shapes=[jax.ShapeDtypeStruct((N, nq), BF16)],
               out_specs=[_row_spec(tn, 1)], epilogue=_epi_head_rope, lag=1, name="gqa_q")[0]
    k = matmul(a, w_qkv, tn=tn, col_off=nq // tn, n_cols=nk, extra=(g_kn.reshape(1, -1), c, s1, s2),
               extra_specs=rope_specs, out_shapes=[jax.ShapeDtypeStruct((N, nk), BF16)],
               out_specs=[_row_spec(tn, 1)], epilogue=_epi_head_rope, lag=1, name="gqa_k")[0]
    v = matmul(a, w_qkv, tn=tn, col_off=(nq + nk) // tn, n_cols=nk,
               out_shapes=[jax.ShapeDtypeStruct((N, nk), BF16)], out_specs=[_row_spec(tn)],
               epilogue=_epi_plain, name="gqa_v")[0]
    o = attention(q, k, v, n_steps=GQA_KV_HEADS, n_heads=GQA_REP, kv_shared=True, dqk=GQA_HD, dv=GQA_HD,
                  with_ctx=with_ctx)
    return _out_proj(_attn_rows(*o), *w_o_args)


def _attn_rows(o_ctx, o_lat):
    if o_ctx is None:
        return o_lat
    width = o_lat.shape[1]
    return jnp.concatenate([o_ctx.reshape(B, CTX, width), o_lat.reshape(B, T, width)], axis=1).reshape(N, width)


def _out_proj(o, w_o, h, mod3, gate_idx, rows):
    tn = 1024
    return matmul(o, w_o, tn=tn, extra=(h, mod3),
                  extra_specs=(pl.BlockSpec((TT, tn), lambda n, i: (rows.src_tile(i), n)),
                               pl.BlockSpec((None, 1, tn), lambda n, i: (rows.mod_row(i) * N_MOD + gate_idx, 0, n))),
                  out_shapes=[jax.ShapeDtypeStruct((rows.n_rows, D), F32)], out_specs=[_row_spec(tn)],
                  epilogue=_epi_resgate, n_rows=rows.n_rows, name="out_proj")[0]


def kernel(x, c, ctx, c_ctx, ada_w, ada_b, norm1_g, norm2_g, mla_w_down, mla_g_q, mla_g_kv, mla_w_uq,
           mla_w_ukv, mla_g_qn, mla_g_kn, mla_w_o, gqa_w_qkv, gqa_g_qn, gqa_g_kn, gqa_w_o, moe_w_router,
           moe_b_router, moe_w_gate, moe_w_up, moe_w_down, moe_ws_gate, moe_ws_up, moe_ws_down):
    depth = ada_w.shape[0]
    h = jnp.concatenate([ctx, x], axis=1).reshape(N, D)
    c_rows = jnp.concatenate([c_ctx[None, :], c, jnp.zeros((16 - 1 - B, D), F32)], axis=0)
    mod = ada_table(c_rows, ada_w, ada_b)
    tables_a = _rope_tables(MLA_ROPE)
    tables_b = _rope_tables(GQA_HD)
    for i in range(depth):
        with_ctx = i < depth - 1
        rows = ALL_ROWS if with_ctx else LATENT_ROWS
        mod3 = mod[i].reshape(16 * N_MOD, 1, D)
        a = norm_mod(h, norm1_g[i], mod3, 0, 1, BF16)
        j = i // 2
        if i % 2 == 0:
            h = mla_mixer(a, tables_a, mla_w_down[j], mla_g_q[j], mla_g_kv[j], mla_w_uq[j], mla_w_ukv[j],
                          mla_g_qn[j], mla_g_kn[j], (mla_w_o[j], h, mod3, 2, rows), with_ctx)
        else:
            h = gqa_mixer(a, tables_b, gqa_w_qkv[j], gqa_g_qn[j], gqa_g_kn[j], (gqa_w_o[j], h, mod3, 2, rows),
                          with_ctx)
        f, f_packed = norm_mod(h, norm2_g[i], mod3, 3, 4, F32, with_packed=True, rows=rows)
        h = moe_layer(h, f, f_packed, mod3, 5, i, rows, moe_w_router[i], moe_b_router[i], moe_w_gate, moe_w_up,
                      moe_w_down, moe_ws_gate, moe_ws_up, moe_ws_down)
    return h.reshape(B, T, D)
```

```python
import functools
from typing import Callable, NamedTuple

import jax
import jax.numpy as jnp
import numpy as np
from jax import lax
from jax.experimental import pallas as pl
from jax.experimental.pallas import tpu as pltpu

F32 = jnp.float32
BF16 = jnp.bfloat16
HIGHEST = lax.Precision.HIGHEST

D = 2048
B = 8
T = 2048
CTX = 256
S = CTX + T
N = B * S
TT = 256
TILES_B = S // TT
GRID_W = 64
ROPE_THETA = 10000.0
EPS = 1e-6
N_MOD = 6

MLA_HEADS = 16
MLA_Q_RANK = 768
MLA_KV_RANK = 512
MLA_NOPE = 128
MLA_ROPE = 64
MLA_QK = MLA_NOPE + MLA_ROPE
MLA_V = 128
MLA_HW = 256

GQA_HEADS = 16
GQA_KV_HEADS = 4
GQA_REP = GQA_HEADS // GQA_KV_HEADS
GQA_HD = 128

N_EXPERTS = 64
N_GROUPS = 8
GROUP_SIZE = N_EXPERTS // N_GROUPS
TOPK_GROUPS = 4
TOP_K = 8
D_EXPERT = 512
ROUTED_SCALE = 2.5
MOE_BLOCK = 256

LANES = 128
LOG2E = 1.4426950408889634
VMEM_LIMIT = 56 * 1024 * 1024


def _params(sem, vmem=VMEM_LIMIT):
    return pltpu.CompilerParams(dimension_semantics=sem, vmem_limit_bytes=vmem)


def _mod_row(i):
    return jnp.where(i % TILES_B == 0, 0, i // TILES_B + 1)


class Rows(NamedTuple):
    n_rows: int
    src_tile: Callable
    mod_row: Callable


LAT_TILES = T // TT
ALL_ROWS = Rows(N, lambda i: i, _mod_row)
LATENT_ROWS = Rows(B * T, lambda i: i // LAT_TILES * TILES_B + 1 + i % LAT_TILES, lambda i: i // LAT_TILES + 1)


def _ada_kernel(c_ref, w_ref, b_ref, o_ref):
    a = c_ref[...]
    a = a * jax.nn.sigmoid(a)
    o_ref[...] = jnp.dot(a, w_ref[...], preferred_element_type=F32, precision=HIGHEST) + b_ref[...]


def ada_table(c_rows, ada_w, ada_b):
    depth = ada_w.shape[0]
    tn = 1024
    return pl.pallas_call(
        _ada_kernel,
        grid=(depth, N_MOD * D // tn),
        in_specs=[
            pl.BlockSpec((16, D), lambda l, n: (0, 0)),
            pl.BlockSpec((None, D, tn), lambda l, n: (l, 0, n)),
            pl.BlockSpec((None, 1, tn), lambda l, n: (l, 0, n)),
        ],
        out_specs=pl.BlockSpec((None, 16, tn), lambda l, n: (l, 0, n)),
        out_shape=jax.ShapeDtypeStruct((depth, 16, N_MOD * D), F32),
        compiler_params=_params(("parallel", "parallel")),
        name="ada_table",
    )(c_rows, ada_w, ada_b.reshape(depth, 1, N_MOD * D))


HALF = D // 2
ROW_TILES = HALF // LANES


def _store_packed(ref, y):
    rows = y.shape[0]
    for s in range(ROW_TILES):
        lo = y[:, s * LANES:(s + 1) * LANES]
        hi = y[:, HALF + s * LANES:HALF + (s + 1) * LANES]
        ref[pl.ds(s, rows, stride=ROW_TILES), :] = pltpu.pack_elementwise([lo, hi], packed_dtype=BF16)


def _load_packed(ref, base, rows, s):
    u = ref[pl.ds(base * ROW_TILES + s, rows, stride=ROW_TILES), :]
    lo = pltpu.unpack_elementwise(u, index=0, packed_dtype=BF16, unpacked_dtype=F32)
    hi = pltpu.unpack_elementwise(u, index=1, packed_dtype=BF16, unpacked_dtype=F32)
    return lo, hi


def _norm_mod_kernel(h_ref, g_ref, sh_ref, sc_ref, o_ref, *packed_ref):
    x = h_ref[...]
    ms = jnp.mean(x * x, axis=-1, keepdims=True)
    y = x * lax.rsqrt(ms + EPS) * g_ref[...]
    y = y * (1.0 + sc_ref[...]) + sh_ref[...]
    o_ref[...] = y.astype(o_ref.dtype)
    if packed_ref:
        _store_packed(packed_ref[0], y)


def norm_mod(h, gain, mod3, shift_idx, scale_idx, out_dtype, with_packed=False, rows=ALL_ROWS):
    n = rows.n_rows
    row = pl.BlockSpec((TT, D), lambda i: (i, 0))
    out_specs, out_shape = [row], [jax.ShapeDtypeStruct((n, D), out_dtype)]
    if with_packed:
        out_specs.append(pl.BlockSpec((TT * ROW_TILES, LANES), lambda i: (i, 0)))
        out_shape.append(jax.ShapeDtypeStruct((n * ROW_TILES, LANES), jnp.uint32))
    outs = pl.pallas_call(
        _norm_mod_kernel,
        grid=(n // TT,),
        in_specs=[
            row,
            pl.BlockSpec((1, D), lambda i: (0, 0)),
            pl.BlockSpec((None, 1, D), lambda i: (rows.mod_row(i) * N_MOD + shift_idx, 0, 0)),
            pl.BlockSpec((None, 1, D), lambda i: (rows.mod_row(i) * N_MOD + scale_idx, 0, 0)),
        ],
        out_specs=out_specs,
        out_shape=out_shape,
        compiler_params=_params(("parallel",)),
        name="norm_mod",
    )(h, gain.reshape(1, D), mod3, mod3)
    return outs if with_packed else outs[0]


def _mm_kernel(*refs, n_extra, n_out, epilogue, lag):
    a_ref, w_ref = refs[0], refs[1]
    extra = refs[2:2 + n_extra]
    outs = refs[2 + n_extra:2 + n_extra + n_out]
    wb_ref = refs[2 + n_extra + n_out]
    i = pl.program_id(1)

    @pl.when(i == 0)
    def _():
        wb_ref[...] = w_ref[...].astype(BF16)

    def product():
        return jnp.dot(a_ref[...].astype(BF16), wb_ref[...], preferred_element_type=F32)

    if not lag:
        epilogue(product(), extra, outs)
        return

    acc0, acc1 = refs[3 + n_extra + n_out:]

    @pl.when(i == 0)
    def _():
        acc1[...] = jnp.zeros_like(acc1)

    for parity, (cur, prev) in enumerate(((acc0, acc1), (acc1, acc0))):
        @pl.when(i % 2 == parity)
        def _():
            epilogue(prev[...], extra, outs)
            cur[...] = product()


def _lagged(i, lag):
    return jnp.maximum(i - lag, 0)


def matmul(a, w, *, tn, col_off=0, n_cols=None, extra=(), extra_specs=(), out_shapes, out_specs, epilogue,
           name, lag=0, n_rows=N):
    k = a.shape[1]
    n_cols = w.shape[1] if n_cols is None else n_cols
    n_tiles = n_rows // TT
    kern = functools.partial(_mm_kernel, n_extra=len(extra), n_out=len(out_shapes), epilogue=epilogue, lag=lag)
    acc = [pltpu.VMEM((TT, tn), F32)] * 2 if lag else []
    return pl.pallas_call(
        kern,
        grid=(n_cols // tn, n_tiles + lag),
        in_specs=[
            pl.BlockSpec((TT, k), lambda n, i: (jnp.minimum(i, n_tiles - 1), 0)),
            pl.BlockSpec((k, tn), lambda n, i: (0, n + col_off)),
            *extra_specs,
        ],
        out_specs=out_specs,
        out_shape=out_shapes,
        scratch_shapes=[pltpu.VMEM((k, tn), BF16), *acc],
        compiler_params=_params(("parallel", "arbitrary")),
        name=name,
    )(a, w, *extra)


def _epi_plain(acc, extra, outs):
    outs[0][...] = acc.astype(outs[0].dtype)


def _rms(x, g_ref):
    return x * lax.rsqrt(jnp.mean(x * x, axis=-1, keepdims=True) + EPS) * g_ref[...]


def _epi_mla_down(acc, extra, outs):
    gq_ref, gkv_ref = extra
    cq_out, ckv_out, rope_out = outs
    kv_end = MLA_Q_RANK + MLA_KV_RANK
    cq_out[...] = _rms(acc[:, :MLA_Q_RANK], gq_ref).astype(cq_out.dtype)
    ckv_out[...] = _rms(acc[:, MLA_Q_RANK:kv_end], gkv_ref).astype(ckv_out.dtype)
    rope_out[...] = acc[:, kv_end:]


def _epi_resgate(acc, extra, outs):
    res_ref, gate_ref = extra
    outs[0][...] = res_ref[...] + gate_ref[...] * acc


def _rope(x, c, s1, s2, quarter):
    return x * c + pltpu.roll(x, LANES - quarter, 1) * s1 + pltpu.roll(x, quarter, 1) * s2


def _epi_head_rope(acc, extra, outs):
    g_ref, c_ref, s1_ref, s2_ref = extra
    c, s1, s2 = c_ref[...], s1_ref[...], s2_ref[...]
    for j in range(acc.shape[1] // LANES):
        x = acc[:, j * LANES:(j + 1) * LANES]
        ms = jnp.mean(x * x, axis=-1, keepdims=True)
        xn = x * lax.rsqrt(ms + EPS) * g_ref[...]
        outs[0][:, j * LANES:(j + 1) * LANES] = _rope(xn, c, s1, s2, GQA_HD // 4).astype(outs[0].dtype)


def _mla_head(nope, rope, g_ref, c, s1, s2):
    ss = jnp.sum(nope * nope, axis=-1, keepdims=True) + jnp.sum(rope * rope, axis=-1, keepdims=True)
    r = lax.rsqrt(ss * (1.0 / MLA_QK) + EPS)
    nope_n = nope * r * g_ref[:, :LANES]
    rope_n = _rope(rope * r * g_ref[:, LANES:], c, s1, s2, MLA_ROPE // 4)
    return nope_n, rope_n


def _epi_mla_q(acc, extra, outs):
    g_ref, c_ref, s1_ref, s2_ref = extra
    c, s1, s2 = c_ref[...], s1_ref[...], s2_ref[...]
    for j in range(acc.shape[1] // MLA_HW):
        nope = acc[:, j * MLA_HW:j * MLA_HW + LANES]
        rope = acc[:, j * MLA_HW + LANES:(j + 1) * MLA_HW]
        nope_n, rope_n = _mla_head(nope, rope, g_ref, c, s1, s2)
        dst = j * MLA_HW
        outs[0][:, dst:dst + LANES] = nope_n.astype(outs[0].dtype)
        outs[0][:, dst + LANES:dst + MLA_HW] = rope_n.astype(outs[0].dtype)


def _epi_mla_kv(acc, extra, outs):
    kr_ref, g_ref, c_ref, s1_ref, s2_ref = extra
    k_out, v_out = outs
    c, s1, s2 = c_ref[...], s1_ref[...], s2_ref[...]
    rope = kr_ref[...]
    for j in range(acc.shape[1] // MLA_HW):
        nope = acc[:, j * MLA_HW:j * MLA_HW + LANES]
        v = acc[:, j * MLA_HW + LANES:(j + 1) * MLA_HW]
        nope_n, rope_n = _mla_head(nope, rope, g_ref, c, s1, s2)
        dst = j * MLA_HW
        k_out[:, dst:dst + LANES] = nope_n.astype(k_out.dtype)
        k_out[:, dst + LANES:dst + MLA_HW] = rope_n.astype(k_out.dtype)
        v_out[:, dst // 2:dst // 2 + LANES] = v.astype(v_out.dtype)


def _row_spec(width, lag=0):
    return pl.BlockSpec((TT, width), lambda n, i: (_lagged(i, lag), n))


def _const_spec(width):
    return pl.BlockSpec((1, width), lambda n, i: (0, 0))


def _table_spec(lag=0):
    return pl.BlockSpec((TT, LANES), lambda n, i: (_lagged(i, lag) % TILES_B, 0))


KEY_CHUNK = 768


def _attn_kernel(*refs, n_heads, kv_shared, dqk, dv, with_ctx):
    if with_ctx:
        qc_ref, ql_ref, k_ref, v_ref, oc_ref, ol_ref = refs
    else:
        ql_ref, k_ref, v_ref, ol_ref = refs

    def compute(q_ref, o_ref, nk):
        chunk = min(KEY_CHUNK, nk)
        n_chunks = nk // chunk

        def scores(h, c):
            kh = 0 if kv_shared else h
            q = q_ref[:, h * dqk:(h + 1) * dqk]
            k = k_ref[c * chunk:(c + 1) * chunk, kh * dqk:(kh + 1) * dqk]
            return lax.dot_general(q, k, (((1,), (1,)), ((), ())), preferred_element_type=F32)

        def row_max(s_chunks):
            m = jnp.max(s_chunks[0], axis=-1, keepdims=True)
            for s in s_chunks[1:]:
                m = jnp.maximum(m, jnp.max(s, axis=-1, keepdims=True))
            return m

        s_cur = [scores(0, c) for c in range(n_chunks)]
        for h in range(n_heads):
            vh = 0 if kv_shared else h
            m = row_max(s_cur)
            s_next = []
            l = None
            o = None
            for c in range(n_chunks):
                p = jnp.exp2(s_cur[c] - m)
                l_c = jnp.sum(p, axis=-1, keepdims=True)
                v = v_ref[c * chunk:(c + 1) * chunk, vh * dv:(vh + 1) * dv]
                o_c = jnp.dot(p.astype(BF16), v, preferred_element_type=F32)
                l = l_c if l is None else l + l_c
                o = o_c if o is None else o + o_c
                if h + 1 < n_heads:
                    s_next.append(scores(h + 1, c))
            o_ref[:, h * dv:(h + 1) * dv] = (o / l).astype(o_ref.dtype)
            s_cur = s_next

    if with_ctx:
        @pl.when(pl.program_id(2) == 0)
        def _():
            compute(qc_ref, oc_ref, CTX)

        @pl.when(pl.program_id(2) > 0)
        def _():
            compute(ql_ref, ol_ref, S)
    else:
        compute(ql_ref, ol_ref, S)


LAT_Q_ROWS = 2 * TT


def attention(q, k, v, *, n_steps, n_heads, kv_shared, dqk, dv, with_ctx):
    kv_heads = 1 if kv_shared else n_heads
    q_cols, o_cols = n_heads * dqk, n_heads * dv
    lat_blocks = T // LAT_Q_ROWS
    first = 1 if with_ctx else 0

    def lat_block(i):
        return jnp.maximum(i - first, 0)

    ql_spec = pl.BlockSpec((pl.Element(LAT_Q_ROWS), pl.Element(q_cols)),
                           lambda b, g, i: (pl.multiple_of(b * S + CTX + lat_block(i) * LAT_Q_ROWS, TT),
                                            pl.multiple_of(g * q_cols, LANES)))
    k_spec = pl.BlockSpec((S, kv_heads * dqk), lambda b, g, i: (b, g))
    v_spec = pl.BlockSpec((S, kv_heads * dv), lambda b, g, i: (b, g))
    ol_spec = pl.BlockSpec((LAT_Q_ROWS, o_cols), lambda b, g, i: (b * lat_blocks + lat_block(i), g))
    ol_shape = jax.ShapeDtypeStruct((B * T, n_steps * o_cols), BF16)
    if with_ctx:
        in_specs = [pl.BlockSpec((CTX, q_cols), lambda b, g, i: (b * TILES_B, g)), ql_spec, k_spec, v_spec]
        out_specs = [pl.BlockSpec((CTX, o_cols), lambda b, g, i: (b, g)), ol_spec]
        out_shape = [jax.ShapeDtypeStruct((B * CTX, n_steps * o_cols), BF16), ol_shape]
        args = (q, q, k, v)
    else:
        in_specs, out_specs, out_shape, args = [ql_spec, k_spec, v_spec], [ol_spec], [ol_shape], (q, k, v)
    kern = functools.partial(_attn_kernel, n_heads=n_heads, kv_shared=kv_shared, dqk=dqk, dv=dv,
                             with_ctx=with_ctx)
    outs = pl.pallas_call(
        kern,
        grid=(B, n_steps, first + lat_blocks),
        in_specs=in_specs,
        out_specs=out_specs,
        out_shape=out_shape,
        compiler_params=_params(("parallel", "parallel", "arbitrary")),
        name="attention",
    )(*args)
    return outs if with_ctx else (None, outs[0])


def _first_index(hit, iota, size):
    return jnp.min(jnp.where(hit, iota, float(size)), axis=0, keepdims=True)


def _router_kernel(f_ref, wr_ref, br_ref, idx_ref, wt_ref, rank_ref, cnt_ref, carry_ref):
    @pl.when(pl.program_id(0) == 0)
    def _():
        carry_ref[...] = jnp.zeros_like(carry_ref)

    neg = -jnp.inf
    logits = lax.dot_general(wr_ref[...], f_ref[...], (((1,), (1,)), ((), ())),
                             preferred_element_type=F32, precision=HIGHEST)
    scores = jax.nn.sigmoid(logits)
    biased = scores + br_ref[...]

    iota_m = lax.broadcasted_iota(jnp.int32, (GROUP_SIZE, TT), 0).astype(F32)
    groups = [biased[g * GROUP_SIZE:(g + 1) * GROUP_SIZE, :] for g in range(N_GROUPS)]
    gs_rows = []
    for blk in groups:
        m1 = jnp.max(blk, axis=0, keepdims=True)
        i1 = _first_index(blk == m1, iota_m, GROUP_SIZE)
        m2 = jnp.max(jnp.where(iota_m == i1, neg, blk), axis=0, keepdims=True)
        gs_rows.append(m1 + m2)
    gs = jnp.concatenate(gs_rows, axis=0)

    iota_g = lax.broadcasted_iota(jnp.int32, gs.shape, 0).astype(F32)
    sel = jnp.zeros(gs.shape, F32)
    cur = gs
    for _ in range(TOPK_GROUPS):
        m = jnp.max(cur, axis=0, keepdims=True)
        hit = iota_g == _first_index(cur == m, iota_g, N_GROUPS)
        sel = jnp.where(hit, 1.0, sel)
        cur = jnp.where(hit, neg, cur)

    cur = jnp.concatenate(
        [jnp.where(sel[g:g + 1, :] > 0.5, groups[g], neg) for g in range(N_GROUPS)], axis=0)
    iota_e = lax.broadcasted_iota(jnp.int32, cur.shape, 0).astype(F32)
    assigned = jnp.zeros(cur.shape, F32)
    w_rows, hits = [], []
    for k in range(TOP_K):
        m = jnp.max(cur, axis=0, keepdims=True)
        first = _first_index(cur == m, iota_e, N_EXPERTS)
        hit = iota_e == first
        idx_ref[k:k + 1, :] = first.astype(jnp.int32)
        w_rows.append(jnp.sum(jnp.where(hit, scores, 0.0), axis=0, keepdims=True))
        hits.append(hit)
        assigned = jnp.where(hit, 1.0, assigned)
        cur = jnp.where(hit, neg, cur)

    w_sum = w_rows[0]
    for k in range(1, TOP_K):
        w_sum = w_sum + w_rows[k]
    for k in range(TOP_K):
        wt_ref[k:k + 1, :] = w_rows[k] / w_sum * ROUTED_SCALE

    r_i = lax.broadcasted_iota(jnp.int32, (TT, TT), 0)
    c_i = lax.broadcasted_iota(jnp.int32, (TT, TT), 1)
    upper = jnp.where(r_i <= c_i, 1.0, 0.0).astype(BF16)
    incl = jnp.dot(assigned.astype(BF16), upper, preferred_element_type=F32)
    rank_e = carry_ref[...] + incl - assigned
    for k in range(TOP_K):
        rank_k = jnp.sum(jnp.where(hits[k], rank_e, 0.0), axis=0, keepdims=True)
        rank_ref[k:k + 1, :] = rank_k.astype(jnp.int32)
    carry = carry_ref[...] + jnp.sum(assigned, axis=1, keepdims=True)
    carry_ref[...] = carry
    cnt_ref[...] = carry.astype(jnp.int32)


def router(f, w_router, b_router):
    n = f.shape[0]
    tok = pl.BlockSpec((TOP_K, TT), lambda i: (0, i))
    return pl.pallas_call(
        _router_kernel,
        grid=(n // TT,),
        in_specs=[
            pl.BlockSpec((TT, D), lambda i: (i, 0)),
            pl.BlockSpec((N_EXPERTS, D), lambda i: (0, 0)),
            pl.BlockSpec((N_EXPERTS, 1), lambda i: (0, 0)),
        ],
        out_specs=[tok, tok, tok, pl.BlockSpec((N_EXPERTS, 1), lambda i: (0, 0))],
        out_shape=[
            jax.ShapeDtypeStruct((TOP_K, n), jnp.int32),
            jax.ShapeDtypeStruct((TOP_K, n), F32),
            jax.ShapeDtypeStruct((TOP_K, n), jnp.int32),
            jax.ShapeDtypeStruct((N_EXPERTS, 1), jnp.int32),
        ],
        scratch_shapes=[pltpu.VMEM((N_EXPERTS, 1), F32)],
        compiler_params=_params(("arbitrary",)),
        name="router",
    )(f, w_router.T, b_router.reshape(N_EXPERTS, 1))


def _swiglu(x, wg, wu, wd):
    g = jnp.dot(x, wg, preferred_element_type=F32)
    u = jnp.dot(x, wu, preferred_element_type=F32)
    mid = (g * jax.nn.sigmoid(g) * u).astype(BF16)
    return jnp.dot(mid, wd, preferred_element_type=F32)


def _packed_rows(ref, first_row, n_rows):
    start = first_row * ROW_TILES
    if not isinstance(start, int):
        start = pl.multiple_of(start, ROW_TILES)
    return ref.at[pl.ds(start, n_rows * ROW_TILES)]


IDX_SLOTS = 4
WEIGHT_DMA_QUEUE = 1
SCATTER_DMA_QUEUE = 1


def _routed_kernel(be_ref, ne_ref, nu_ref, idx_hbm, x_hbm, wg_hbm, wu_hbm, wd_hbm, ys_hbm,
                   idx_smem, xbuf0, xbuf1, ybuf0, ybuf1, wgs, wus, wds, wgb, wub, wdb,
                   sem_i, sem_g, sem_s, sem_w, *, layer):
    j = pl.program_id(0)
    n_used = nu_ref[0]
    n_blocks = pl.num_programs(0)
    xbufs = (xbuf0, xbuf1)
    ybufs = (ybuf0, ybuf1)

    def idx_copy(blk):
        s = blk % IDX_SLOTS
        return pltpu.make_async_copy(idx_hbm.at[jnp.minimum(blk, n_blocks - 1)], idx_smem.at[s], sem_i.at[s])

    def weight_copies(e):
        return [pltpu.make_async_copy(src.at[layer, e], dst, sem_w)
                for src, dst in ((wg_hbm, wgs), (wu_hbm, wus), (wd_hbm, wds))]

    def gather_start(t, p, r):
        pltpu.make_async_copy(_packed_rows(x_hbm, t, 1), _packed_rows(xbufs[p], r, 1), sem_g.at[p]).start()

    def scatter_start(d, p, r):
        pltpu.make_async_copy(_packed_rows(ybufs[p], r, 1), _packed_rows(ys_hbm, d, 1),
                              sem_s.at[p]).start(priority=SCATTER_DMA_QUEUE)

    def gather_wait(p):
        pltpu.make_async_copy(_packed_rows(x_hbm, 0, MOE_BLOCK), _packed_rows(xbufs[p], 0, MOE_BLOCK),
                              sem_g.at[p]).wait()

    def scatter_wait(p):
        pltpu.make_async_copy(_packed_rows(ybufs[p], 0, MOE_BLOCK), _packed_rows(ys_hbm, 0, MOE_BLOCK),
                              sem_s.at[p]).wait()

    @pl.when(jnp.logical_and(j == 0, n_used > 0))
    def _():
        first = idx_copy(0)
        first.start()
        first.wait()

        def issue(r, carry):
            gather_start(idx_smem[0, 0, r], 0, r)
            return carry

        lax.fori_loop(0, MOE_BLOCK, issue, 0, unroll=8)
        idx_copy(1).start()
        spare = pltpu.make_async_copy(idx_hbm.at[n_blocks], idx_smem.at[IDX_SLOTS - 1], sem_i.at[IDX_SLOTS - 1])
        spare.start()
        spare.wait()
        ybuf1[...] = jnp.zeros_like(ybuf1)
        for cp in weight_copies(be_ref[0]):
            cp.start(priority=WEIGHT_DMA_QUEUE)

    @pl.when(j >= n_used)
    def _():
        ybuf0[...] = jnp.zeros_like(ybuf0)
        fill = pltpu.make_async_copy(ybuf0, _packed_rows(ys_hbm, j * MOE_BLOCK, MOE_BLOCK), sem_s.at[0])
        fill.start()
        fill.wait()

    @pl.when(j < n_used)
    def _():
        e = be_ref[j]

        @pl.when(jnp.logical_or(j == 0, e != be_ref[jnp.maximum(j - 1, 0)]))
        def _():
            for cp in weight_copies(e):
                cp.wait()
            wgb[...] = wgs[...].astype(BF16)
            wub[...] = wus[...].astype(BF16)
            wdb[...] = wds[...].astype(BF16)

            @pl.when(ne_ref[j] != e)
            def _():
                for cp in weight_copies(ne_ref[j]):
                    cp.start(priority=WEIGHT_DMA_QUEUE)

        for cur in (0, 1):
            nxt = 1 - cur

            @pl.when(j % 2 == cur)
            def _():
                @pl.when(j >= 1)
                def _():
                    scatter_wait(cur)

                idx_copy(j + 1).wait()
                gather_wait(cur)
                g_slot = (j + 1) % IDX_SLOTS
                s_slot = (j + IDX_SLOTS - 1) % IDX_SLOTS
                for r in range(MOE_BLOCK):
                    gather_start(idx_smem[g_slot, 0, r], nxt, r)
                    scatter_start(idx_smem[s_slot, 1, r], nxt, r)
                pieces = [_load_packed(xbufs[cur], 0, MOE_BLOCK, s) for s in range(ROW_TILES)]
                x = jnp.concatenate([p[0].astype(BF16) for p in pieces] + [p[1].astype(BF16) for p in pieces],
                                    axis=1)
                _store_packed(ybufs[cur], _swiglu(x, wgb[...], wub[...], wdb[...]))
                idx_copy(j + 2).start()

                @pl.when(j == n_used - 1)
                def _():
                    def issue(r, carry):
                        scatter_start(idx_smem[j % IDX_SLOTS, 1, r], cur, r)
                        return carry

                    lax.fori_loop(0, MOE_BLOCK, issue, 0, unroll=8)
                    scatter_wait(nxt)
                    scatter_wait(cur)
                    gather_wait(nxt)
                    idx_copy(j + 2).wait()


def routed_experts(x_packed, idx_rows, block_expert, next_expert, n_used, wg, wu, wd, layer, n_out_rows):
    n_blocks = block_expert.shape[0]
    dh = wg.shape[3]
    any_spec = pl.BlockSpec(memory_space=pl.ANY)
    row_buf = pltpu.VMEM((MOE_BLOCK * ROW_TILES, LANES), jnp.uint32)
    grid_spec = pltpu.PrefetchScalarGridSpec(
        num_scalar_prefetch=3,
        grid=(n_blocks,),
        in_specs=[any_spec] * 5,
        out_specs=any_spec,
        scratch_shapes=[
            pltpu.SMEM((IDX_SLOTS, 2, MOE_BLOCK), jnp.int32),
            row_buf, row_buf, row_buf, row_buf,
            pltpu.VMEM((D, dh), F32), pltpu.VMEM((D, dh), F32), pltpu.VMEM((dh, D), F32),
            pltpu.VMEM((D, dh), BF16), pltpu.VMEM((D, dh), BF16), pltpu.VMEM((dh, D), BF16),
            pltpu.SemaphoreType.DMA((IDX_SLOTS,)), pltpu.SemaphoreType.DMA((2,)), pltpu.SemaphoreType.DMA((2,)),
            pltpu.SemaphoreType.DMA,
        ],
    )
    return pl.pallas_call(
        functools.partial(_routed_kernel, layer=layer),
        grid_spec=grid_spec,
        out_shape=jax.ShapeDtypeStruct((n_out_rows * ROW_TILES, LANES), jnp.uint32),
        compiler_params=_params(("arbitrary",)),
        name="routed_experts",
    )(block_expert, next_expert, n_used, idx_rows, x_packed, wg, wu, wd)


COMBINE_T = 128


def _combine_kernel(ys_ref, wt_ref, f_ref, wg_ref, wu_ref, wd_ref, h_ref, gate_ref, o_ref, wgb, wub, wdb):
    @pl.when(pl.program_id(0) == 0)
    def _():
        wgb[...] = wg_ref[...].astype(BF16)
        wub[...] = wu_ref[...].astype(BF16)
        wdb[...] = wd_ref[...].astype(BF16)

    shared = _swiglu(f_ref[...].astype(BF16), wgb[...], wub[...], wdb[...])
    wts = [wt_ref[:, k:k + 1] for k in range(TOP_K)]
    for s in range(ROW_TILES):
        c_lo = slice(s * LANES, (s + 1) * LANES)
        c_hi = slice(HALF + s * LANES, HALF + (s + 1) * LANES)
        acc_lo = shared[:, c_lo]
        acc_hi = shared[:, c_hi]
        for k in range(TOP_K):
            lo, hi = _load_packed(ys_ref, k * COMBINE_T, COMBINE_T, s)
            acc_lo = acc_lo + wts[k] * lo
            acc_hi = acc_hi + wts[k] * hi
        o_ref[:, c_lo] = h_ref[:, c_lo] + gate_ref[:, c_lo] * acc_lo
        o_ref[:, c_hi] = h_ref[:, c_hi] + gate_ref[:, c_hi] * acc_hi


def combine(ys, wts, f, wsg, wsu, wsd, layer, h, mod3, gate_idx, rows):
    per_tt = TT // COMBINE_T
    dh = wsg.shape[2]
    row = pl.BlockSpec((COMBINE_T, D), lambda i: (i, 0))
    return pl.pallas_call(
        _combine_kernel,
        grid=(rows.n_rows // COMBINE_T,),
        in_specs=[
            pl.BlockSpec((COMBINE_T * TOP_K * ROW_TILES, LANES), lambda i: (i, 0)),
            pl.BlockSpec((COMBINE_T, TOP_K), lambda i: (i, 0)),
            row,
            pl.BlockSpec((None, D, dh), lambda i: (layer, 0, 0)),
            pl.BlockSpec((None, D, dh), lambda i: (layer, 0, 0)),
            pl.BlockSpec((None, dh, D), lambda i: (layer, 0, 0)),
            row,
            pl.BlockSpec((None, 1, D), lambda i: (rows.mod_row(i // per_tt) * N_MOD + gate_idx, 0, 0)),
        ],
        out_specs=row,
        out_shape=jax.ShapeDtypeStruct((rows.n_rows, D), F32),
        scratch_shapes=[pltpu.VMEM((D, dh), BF16), pltpu.VMEM((D, dh), BF16), pltpu.VMEM((dh, D), BF16)],
        compiler_params=_params(("arbitrary",)),
        name="combine",
    )(ys, wts, f, wsg, wsu, wsd, h, mod3)


def moe_layer(h, f, f_packed, mod3, gate_idx, layer, rows, w_router, b_router, wg, wu, wd, wsg, wsu, wsd):
    n = rows.n_rows
    idx, wts, rank, counts = router(f, w_router, b_router)
    counts = counts[:, 0]
    padded = (counts + MOE_BLOCK - 1) // MOE_BLOCK * MOE_BLOCK
    pad_end = jnp.cumsum(padded)
    pad_start = pad_end - padded
    onehot = idx[:, :, None] == jnp.arange(N_EXPERTS, dtype=jnp.int32)[None, None, :]
    dest = jnp.sum(jnp.where(onehot, pad_start[None, None, :], 0), axis=-1) + rank
    n_blocks = -(-(n * TOP_K + N_EXPERTS * (MOE_BLOCK - 1)) // MOE_BLOCK)
    cap = n_blocks * MOE_BLOCK
    n_real = n * TOP_K
    e_ids = jnp.arange(N_EXPERTS, dtype=jnp.int32)
    block_start = jnp.arange(n_blocks, dtype=jnp.int32) * MOE_BLOCK
    block_expert = jnp.sum(block_start[:, None] >= pad_end[None, :], axis=1)
    block_expert = jnp.minimum(block_expert, N_EXPERTS - 1).astype(jnp.int32)
    n_used = (pad_end[-1:] // MOE_BLOCK).astype(jnp.int32)
    tile_rows = TOP_K * COMBINE_T
    tok = jnp.arange(n, dtype=jnp.int32)[None, :]
    out_row = tok // COMBINE_T * tile_rows + jnp.arange(TOP_K, dtype=jnp.int32)[:, None] * COMBINE_T + tok % COMBINE_T
    n_pads = cap - n_real
    pads = padded - counts
    pad_cum = jnp.cumsum(pads)
    p = jnp.arange(n_pads, dtype=jnp.int32)
    owner = p[:, None] >= pad_cum[None, :]
    pad_expert = jnp.sum(owner, axis=1)
    is_owner = pad_expert[:, None] == e_ids[None, :]
    first_pad = pad_start + counts - (pad_cum - pads)
    pad_pos = jnp.where(pad_expert == N_EXPERTS, pad_end[-1] - pad_cum[-1],
                        jnp.sum(jnp.where(is_owner, first_pad[None, :], 0), axis=1)) + p
    keys = jnp.concatenate([dest.reshape(-1), pad_pos])
    vals = jnp.concatenate([out_row.reshape(-1), n_real + p])
    out_rows = lax.sort_key_val(keys, vals)[1]
    pos = jnp.arange(cap, dtype=jnp.int32)
    src_tok = jnp.where(out_rows >= n_real, pos % n, out_rows // tile_rows * COMBINE_T + out_rows % COMBINE_T)
    idx_rows = jnp.stack([src_tok.reshape(n_blocks, MOE_BLOCK), out_rows.reshape(n_blocks, MOE_BLOCK)], axis=1)
    last_rows = cap - MOE_BLOCK + jnp.arange(MOE_BLOCK, dtype=jnp.int32)
    idx_rows = jnp.concatenate([idx_rows, jnp.stack([last_rows % n, last_rows])[None]], axis=0)
    later = jnp.logical_and(e_ids[None, :] > e_ids[:, None], counts[None, :] > 0)
    next_by_expert = jnp.min(jnp.where(later, e_ids[None, :], N_EXPERTS), axis=1)
    next_by_expert = jnp.where(next_by_expert == N_EXPERTS, e_ids, next_by_expert)
    next_expert = jnp.sum(jnp.where(block_expert[:, None] == e_ids[None, :], next_by_expert[None, :], 0),
                          axis=1).astype(jnp.int32)
    ys = routed_experts(f_packed, idx_rows, block_expert, next_expert, n_used, wg, wu, wd, layer, cap)
    return combine(ys, wts.T, f, wsg, wsu, wsd, layer, h, mod3, gate_idx, rows)


def _rope_tables(rot_dim):
    rows = T // GRID_W
    row = np.repeat(np.arange(rows, dtype=np.float32), GRID_W)
    col = np.tile(np.arange(GRID_W, dtype=np.float32), rows)
    half = rot_dim // 2
    inv_freq = jnp.asarray(ROPE_THETA, F32) ** (-jnp.arange(0, half, 2, dtype=F32) / half)
    ang_r = jnp.asarray(row)[:, None] * inv_freq[None, :]
    ang_c = jnp.asarray(col)[:, None] * inv_freq[None, :]
    ang = jnp.concatenate([ang_r, ang_r, ang_c, ang_c], axis=-1)
    cos, sin = jnp.cos(ang), jnp.sin(ang)
    quarter = (np.arange(rot_dim) // (rot_dim // 4)) % 2
    s1 = jnp.where(quarter[None, :] == 0, -sin, 0.0)
    s2 = jnp.where(quarter[None, :] == 1, sin, 0.0)

    def full(tbl, fill):
        tbl = jnp.pad(tbl, ((0, 0), (0, LANES - rot_dim)), constant_values=fill)
        return jnp.concatenate([jnp.full((CTX, LANES), fill, F32), tbl], axis=0)

    return full(cos, 1.0), full(s1, 0.0), full(s2, 0.0)


def _pad_heads(w, n_heads, width, new_width):
    k = w.shape[0]
    w = w.reshape(k, n_heads, width)
    return jnp.pad(w, ((0, 0), (0, 0), (0, new_width - width))).reshape(k, n_heads * new_width)


def _pad_gain(g, new_width):
    return jnp.pad(g, (0, new_width - g.shape[0])).reshape(1, new_width)


def mla_mixer(a, tables, w_down, g_q, g_kv, w_uq, w_ukv, g_qn, g_kn, w_o_args, with_ctx):
    c, s1, s2 = tables
    w_down_p = jnp.pad(w_down, ((0, 0), (0, LANES - MLA_ROPE)))
    cq, ckv, k_rope = matmul(
        a, w_down_p, tn=w_down_p.shape[1], extra=(g_q.reshape(1, -1), g_kv.reshape(1, -1)),
        extra_specs=(_const_spec(MLA_Q_RANK), _const_spec(MLA_KV_RANK)),
        out_shapes=[jax.ShapeDtypeStruct((N, MLA_Q_RANK), BF16), jax.ShapeDtypeStruct((N, MLA_KV_RANK), BF16),
                    jax.ShapeDtypeStruct((N, LANES), F32)],
        out_specs=[_row_spec(MLA_Q_RANK), _row_spec(MLA_KV_RANK), _row_spec(LANES)],
        epilogue=_epi_mla_down, name="mla_down")

    tn = 1024
    scale = MLA_QK ** -0.5 * LOG2E
    w_uq_p = _pad_heads(w_uq, MLA_HEADS, MLA_QK, MLA_HW)
    rope_specs = (_const_spec(MLA_HW), _table_spec(1), _table_spec(1), _table_spec(1))
    q = matmul(cq, w_uq_p, tn=tn, extra=(_pad_gain(g_qn * scale, MLA_HW), c, s1, s2), extra_specs=rope_specs,
               out_shapes=[jax.ShapeDtypeStruct((N, MLA_HEADS * MLA_HW), BF16)], out_specs=[_row_spec(tn, 1)],
               epilogue=_epi_mla_q, lag=1, name="mla_up_q")[0]
    k, v = matmul(ckv, w_ukv, tn=tn, extra=(k_rope, _pad_gain(g_kn, MLA_HW), c, s1, s2),
                  extra_specs=(pl.BlockSpec((TT, LANES), lambda n, i: (_lagged(i, 1), 0)), *rope_specs),
                  out_shapes=[jax.ShapeDtypeStruct((N, MLA_HEADS * MLA_HW), BF16),
                              jax.ShapeDtypeStruct((N, MLA_HEADS * MLA_V), BF16)],
                  out_specs=[_row_spec(tn, 1), _row_spec(tn // 2, 1)],
                  epilogue=_epi_mla_kv, lag=1, name="mla_up_kv")
    heads_step = 4
    o = attention(q, k, v, n_steps=MLA_HEADS // heads_step, n_heads=heads_step, kv_shared=False,
                  dqk=MLA_HW, dv=MLA_V, with_ctx=with_ctx)
    return _out_proj(_attn_rows(*o), *w_o_args)


def gqa_mixer(a, tables, w_qkv, g_qn, g_kn, w_o_args, with_ctx):
    c, s1, s2 = tables
    tn = 512
    scale = GQA_HD ** -0.5 * LOG2E
    rope_specs = (_const_spec(GQA_HD), _table_spec(1), _table_spec(1), _table_spec(1))
    nq = GQA_HEADS * GQA_HD
    nk = GQA_KV_HEADS * GQA_HD
    q = matmul(a, w_qkv, tn=tn, n_cols=nq, extra=((g_qn * scale).reshape(1, -1), c, s1, s2),
               extra_specs=rope_specs, out_shapes=[jax.ShapeDtypeStruct((N, nq), BF16)],
               out_specs=[_row_spec(tn, 1)], epilogue=_epi_head_rope, lag=1, name="gqa_q")[0]
    k = matmul(a, w_qkv, tn=tn, col_off=nq // tn, n_cols=nk, extra=(g_kn.reshape(1, -1), c, s1, s2),
               extra_specs=rope_specs, out_shapes=[jax.ShapeDtypeStruct((N, nk), BF16)],
               out_specs=[_row_spec(tn, 1)], epilogue=_epi_head_rope, lag=1, name="gqa_k")[0]
    v = matmul(a, w_qkv, tn=tn, col_off=(nq + nk) // tn, n_cols=nk,
               out_shapes=[jax.ShapeDtypeStruct((N, nk), BF16)], out_specs=[_row_spec(tn)],
               epilogue=_epi_plain, name="gqa_v")[0]
    o = attention(q, k, v, n_steps=GQA_KV_HEADS, n_heads=GQA_REP, kv_shared=True, dqk=GQA_HD, dv=GQA_HD,
                  with_ctx=with_ctx)
    return _out_proj(_attn_rows(*o), *w_o_args)


def _attn_rows(o_ctx, o_lat):
    if o_ctx is None:
        return o_lat
    width = o_lat.shape[1]
    return jnp.concatenate([o_ctx.reshape(B, CTX, width), o_lat.reshape(B, T, width)], axis=1).reshape(N, width)


def _out_proj(o, w_o, h, mod3, gate_idx, rows):
    tn = 1024
    return matmul(o, w_o, tn=tn, extra=(h, mod3),
                  extra_specs=(pl.BlockSpec((TT, tn), lambda n, i: (rows.src_tile(i), n)),
                               pl.BlockSpec((None, 1, tn), lambda n, i: (rows.mod_row(i) * N_MOD + gate_idx, 0, n))),
                  out_shapes=[jax.ShapeDtypeStruct((rows.n_rows, D), F32)], out_specs=[_row_spec(tn)],
                  epilogue=_epi_resgate, n_rows=rows.n_rows, name="out_proj")[0]


def kernel(x, c, ctx, c_ctx, ada_w, ada_b, norm1_g, norm2_g, mla_w_down, mla_g_q, mla_g_kv, mla_w_uq,
           mla_w_ukv, mla_g_qn, mla_g_kn, mla_w_o, gqa_w_qkv, gqa_g_qn, gqa_g_kn, gqa_w_o, moe_w_router,
           moe_b_router, moe_w_gate, moe_w_up, moe_w_down, moe_ws_gate, moe_ws_up, moe_ws_down):
    depth = ada_w.shape[0]
    h = jnp.concatenate([ctx, x], axis=1).reshape(N, D)
    c_rows = jnp.concatenate([c_ctx[None, :], c, jnp.zeros((16 - 1 - B, D), F32)], axis=0)
    mod = ada_table(c_rows, ada_w, ada_b)
    tables_a = _rope_tables(MLA_ROPE)
    tables_b = _rope_tables(GQA_HD)
    for i in range(depth):
        with_ctx = i < depth - 1
        rows = ALL_ROWS if with_ctx else LATENT_ROWS
        mod3 = mod[i].reshape(16 * N_MOD, 1, D)
        a = norm_mod(h, norm1_g[i], mod3, 0, 1, BF16)
        j = i // 2
        if i % 2 == 0:
            h = mla_mixer(a, tables_a, mla_w_down[j], mla_g_q[j], mla_g_kv[j], mla_w_uq[j], mla_w_ukv[j],
                          mla_g_qn[j], mla_g_kn[j], (mla_w_o[j], h, mod3, 2, rows), with_ctx)
        else:
            h = gqa_mixer(a, tables_b, gqa_w_qkv[j], gqa_g_qn[j], gqa_g_kn[j], (gqa_w_o[j], h, mod3, 2, rows),
                          with_ctx)
        f, f_packed = norm_mod(h, norm2_g[i], mod3, 3, 4, F32, with_packed=True, rows=rows)
        h = moe_layer(h, f, f_packed, mod3, 5, i, rows, moe_w_router[i], moe_b_router[i], moe_w_gate, moe_w_up,
                      moe_w_down, moe_ws_gate, moe_ws_up, moe_ws_down)
    return h.reshape(B, T, D)
```
